```python
import jax, jax.numpy as jnp
from jax import lax
import numpy as np

D_MODEL = 1024
BATCH = 4
SEQ = 8192
DEPTH = 4

FOURIER_WIDTH = D_MODEL // 2
FOURIER_GROUPS = 4
CONV_WIDTH = D_MODEL // 2
CONV_KERNEL = 31
HEAD_DIM = 64
N_Q_HEADS = D_MODEL // HEAD_DIM
N_KV_HEADS = N_Q_HEADS // 4
WINDOW = 128
BLOCK = 128
ROPE_THETA = 10000.0
N_EXPERTS = 16
EXPERT_FF = D_MODEL
CAPACITY_FACTOR = 2
EPS = 1e-6
N_EVEN = (DEPTH + 1) // 2
N_ODD = DEPTH // 2

kernel_name = "hybrid_fnet_conformer_swa_ecmoe_encoder"

F32 = jnp.float32


def rms_norm(x, g):
    xf = x.astype(F32)
    y = xf * lax.rsqrt(jnp.mean(xf * xf, axis=-1, keepdims=True) + EPS)
    return (y * g.astype(F32)).astype(x.dtype)


def layer_norm(x, g, b):
    xf = x.astype(F32)
    mu = jnp.mean(xf, axis=-1, keepdims=True)
    var = jnp.mean(jnp.square(xf - mu), axis=-1, keepdims=True)
    y = (xf - mu) * lax.rsqrt(var + EPS)
    return (y * g.astype(F32) + b.astype(F32)).astype(x.dtype)


def adaln(cond, w, b):
    mod = cond @ w + b
    shift, scale, gate = jnp.split(mod, 3, axis=-1)
    return shift[:, None, :], scale[:, None, :], gate[:, None, :]


def rope_tables(positions):
    inv = ROPE_THETA ** (-jnp.arange(0, HEAD_DIM, 2, dtype=F32) / HEAD_DIM)
    ang = positions.astype(F32)[..., None] * inv
    return jnp.cos(ang), jnp.sin(ang)


def apply_rope(t, cos, sin):
    half = HEAD_DIM // 2
    t1, t2 = t[..., :half].astype(F32), t[..., half:].astype(F32)
    c, s = cos[:, :, None, :], sin[:, :, None, :]
    return jnp.concatenate([t1 * c - t2 * s, t2 * c + t1 * s], axis=-1).astype(t.dtype)


def fourier_conv_mixer(h, w_in, conv_w, conv_b, ln_g, ln_b, w_out):
    B, S, _ = h.shape
    proj = h @ w_in
    u_f = proj[..., :FOURIER_WIDTH]
    u_v = proj[..., FOURIER_WIDTH:FOURIER_WIDTH + CONV_WIDTH]
    u_g = proj[..., FOURIER_WIDTH + CONV_WIDTH:]
    uf = u_f.astype(F32).reshape(B, S, FOURIER_GROUPS, FOURIER_WIDTH // FOURIER_GROUPS)
    y_f = jnp.fft.fft2(uf, axes=(1, 3), norm="ortho").real.reshape(B, S, FOURIER_WIDTH).astype(h.dtype)
    glu = u_v * jax.nn.sigmoid(u_g)
    pad = CONV_KERNEL // 2
    conv = lax.conv_general_dilated(
        glu, conv_w[:, None, :], window_strides=(1,), padding=[(pad, pad)],
        dimension_numbers=("NWC", "WIO", "NWC"), feature_group_count=CONV_WIDTH) + conv_b
    y_c = jax.nn.silu(layer_norm(conv, ln_g, ln_b))
    return jnp.concatenate([y_f, y_c], axis=-1) @ w_out


def windowed_attention(h, w_qkv, sink, w_out, cos, sin):
    B, S, _ = h.shape
    nb = S // BLOCK
    G = N_Q_HEADS // N_KV_HEADS
    qd, kd = N_Q_HEADS * HEAD_DIM, N_KV_HEADS * HEAD_DIM
    qkv = h @ w_qkv
    q = apply_rope(qkv[..., :qd].reshape(B, S, N_Q_HEADS, HEAD_DIM), cos, sin)
    k = apply_rope(qkv[..., qd:qd + kd].reshape(B, S, N_KV_HEADS, HEAD_DIM), cos, sin)
    v = qkv[..., qd + kd:].reshape(B, S, N_KV_HEADS, HEAD_DIM)
    qb = q.reshape(B, nb, BLOCK, N_KV_HEADS, G, HEAD_DIM)

    def band(t):
        tp = jnp.pad(t, ((0, 0), (BLOCK, BLOCK), (0, 0), (0, 0))).reshape(B, nb + 2, BLOCK, N_KV_HEADS, HEAD_DIM)
        return jnp.concatenate([tp[:, :-2], tp[:, 1:-1], tp[:, 2:]], axis=2)

    kb, vb = band(k), band(v)
    s = jnp.einsum("bnqkgd,bnjkd->bnkgqj", qb, kb, preferred_element_type=F32) * (HEAD_DIM ** -0.5)
    qi = jnp.arange(BLOCK)[:, None]
    kj = jnp.arange(3 * BLOCK)[None, :]
    rel = kj - BLOCK - qi
    kpos = (jnp.arange(nb)[:, None, None] - 1) * BLOCK + kj[None]
    valid = (jnp.abs(rel) <= WINDOW)[None] & (kpos >= 0) & (kpos < S)
    s = jnp.where(valid[None, :, None, None], s, -jnp.inf)
    sink_l = sink.astype(F32).reshape(N_KV_HEADS, G)[None, None, :, :, None, None]
    m = jnp.maximum(jnp.max(s, axis=-1, keepdims=True), sink_l)
    e = jnp.exp(s - m)
    p = e / (jnp.sum(e, axis=-1, keepdims=True) + jnp.exp(sink_l - m))
    o = jnp.einsum("bnkgqj,bnjkd->bnqkgd", p.astype(h.dtype), vb).reshape(B, S, qd)
    return o @ w_out


def ec_moe(h, router_w, router_b, w_gate, w_up, w_down):
    B, S, _ = h.shape
    cap = CAPACITY_FACTOR * S // N_EXPERTS
    logits = jnp.einsum("bsd,de->bse", h, router_w, preferred_element_type=F32) + router_b.astype(F32)
    aff = jax.nn.softmax(logits, axis=-1)
    gate, idx = lax.top_k(jnp.swapaxes(aff, 1, 2), cap)
    bidx = jnp.arange(B)[:, None, None]
    xg = h[bidx, idx]
    hid = jax.nn.silu(jnp.einsum("becd,edf->becf", xg, w_gate)) * jnp.einsum("becd,edf->becf", xg, w_up)
    y = jnp.einsum("becf,efd->becd", hid, w_down) * gate[..., None].astype(h.dtype)
    return jnp.zeros_like(h).at[bidx, idx].add(y)


def setup_inputs(seed: int = 0) -> dict:
    key = jax.random.key(seed)
    ks = jax.random.split(key, 24)
    D, E, F = D_MODEL, N_EXPERTS, EXPERT_FF
    n = lambda k, shape, sc: jax.random.normal(k, shape, F32) * sc
    fw_in = FOURIER_WIDTH + 2 * CONV_WIDTH
    mix_out = FOURIER_WIDTH + CONV_WIDTH
    qkv_dim = (N_Q_HEADS + 2 * N_KV_HEADS) * HEAD_DIM
    offset = jax.random.randint(ks[2], (BATCH, 1), 0, 1024, dtype=jnp.int32)
    positions = offset + jnp.arange(SEQ, dtype=jnp.int32)[None, :]
    return {
        "x": n(ks[0], (BATCH, SEQ, D), 1.0),
        "c": n(ks[1], (BATCH, D), 1.0),
        "positions": positions,
        "ada_w": n(ks[3], (DEPTH, 2, D, 3 * D), 0.5 * D ** -0.5),
        "ada_b": n(ks[4], (DEPTH, 2, 3 * D), 0.02),
        "mix_norm_g": 1.0 + n(ks[5], (DEPTH, D), 0.02),
        "ffn_norm_g": 1.0 + n(ks[6], (DEPTH, D), 0.02),
        "fc_w_in": n(ks[7], (N_EVEN, D, fw_in), D ** -0.5),
        "conv_w": n(ks[8], (N_EVEN, CONV_KERNEL, CONV_WIDTH), CONV_KERNEL ** -0.5),
        "conv_b": n(ks[9], (N_EVEN, CONV_WIDTH), 0.02),
        "conv_ln_g": 1.0 + n(ks[10], (N_EVEN, CONV_WIDTH), 0.02),
        "conv_ln_b": n(ks[11], (N_EVEN, CONV_WIDTH), 0.02),
        "fc_w_out": n(ks[12], (N_EVEN, mix_out, D), mix_out ** -0.5),
        "attn_w_qkv": n(ks[13], (N_ODD, D, qkv_dim), D ** -0.5),
        "attn_sink": n(ks[14], (N_ODD, N_Q_HEADS), 1.0),
        "attn_w_out": n(ks[15], (N_ODD, N_Q_HEADS * HEAD_DIM, D), (N_Q_HEADS * HEAD_DIM) ** -0.5),
        "router_w": n(ks[16], (DEPTH, D, E), D ** -0.5),
        "router_b": n(ks[17], (DEPTH, E), 0.01),
        "moe_w_gate": n(ks[18], (DEPTH, E, D, F), D ** -0.5),
        "moe_w_up": n(ks[19], (DEPTH, E, D, F), D ** -0.5),
        "moe_w_down": n(ks[20], (DEPTH, E, F, D), F ** -0.5),
        "final_norm_g": 1.0 + n(ks[21], (D,), 0.02),
    }


def reference(x, c, positions, ada_w, ada_b, mix_norm_g, ffn_norm_g, fc_w_in, conv_w, conv_b,
              conv_ln_g, conv_ln_b, fc_w_out, attn_w_qkv, attn_sink, attn_w_out, router_w,
              router_b, moe_w_gate, moe_w_up, moe_w_down, final_norm_g):
    cos, sin = rope_tables(positions)
    cond = jax.nn.silu(c)
    for l in range(DEPTH):
        i = l // 2
        shift, scale, gate = adaln(cond, ada_w[l, 0], ada_b[l, 0])
        h = rms_norm(x, mix_norm_g[l]) * (1 + scale) + shift
        if l % 2 == 0:
            y = fourier_conv_mixer(h, fc_w_in[i], conv_w[i], conv_b[i], conv_ln_g[i], conv_ln_b[i], fc_w_out[i])
        else:
            y = windowed_attention(h, attn_w_qkv[i], attn_sink[i], attn_w_out[i], cos, sin)
        x = x + gate * y
        shift, scale, gate = adaln(cond, ada_w[l, 1], ada_b[l, 1])
        h = rms_norm(x, ffn_norm_g[l]) * (1 + scale) + shift
        x = x + gate * ec_moe(h, router_w[l], router_b[l], moe_w_gate[l], moe_w_up[l], moe_w_down[l])
    return rms_norm(x, final_norm_g)
```

```python
import functools

import numpy as np
import jax
import jax.numpy as jnp
from jax import lax
from jax.experimental import pallas as pl
from jax.experimental.pallas import tpu as pltpu

F32 = jnp.float32
BF16 = jnp.bfloat16
I32 = jnp.int32

EPS = 1e-6
HEAD_DIM = 64
N_Q_HEADS = 16
N_KV_HEADS = 4
GQA = N_Q_HEADS // N_KV_HEADS
BLOCK = 128
N_EXPERTS = 16
CAPACITY_FACTOR = 2
CONV_KERNEL = 31
FOURIER_WIDTH = 512
CONV_WIDTH = 512
GROUP_CH = 128
ROPE_THETA = 10000.0
LANES = 128
SUBLANES = 8
FFT_N2 = 256
HALO = 16
VMEM_LIMIT = 56 * 1024 * 1024


def _cparams(sem):
    return pltpu.CompilerParams(dimension_semantics=sem, vmem_limit_bytes=VMEM_LIMIT)


def _rms_mod(x, g, scale, shift):
    y = x * lax.rsqrt(jnp.mean(x * x, axis=-1, keepdims=True) + EPS)
    return (y * g) * (1.0 + scale) + shift


def _adaln_kernel(ct_ref, w_ref, b_ref, o_ref):
    ct = ct_ref[...]
    cond = ct * jax.nn.sigmoid(ct)
    w = w_ref[...]
    rows = [jnp.sum(w * cond[:, b:b + 1], axis=0, keepdims=True) for b in range(ct.shape[1])]
    o_ref[...] = jnp.concatenate(rows, axis=0) + b_ref[...]


def adaln_all(c, ada_w, ada_b):
    B, D = c.shape
    L, _, N = ada_w.shape
    tn = 1024
    return pl.pallas_call(
        _adaln_kernel,
        grid=(L, N // tn),
        in_specs=[
            pl.BlockSpec((D, B), lambda l, j: (0, 0)),
            pl.BlockSpec((None, D, tn), lambda l, j: (l, 0, j)),
            pl.BlockSpec((None, 1, tn), lambda l, j: (l, 0, j)),
        ],
        out_specs=pl.BlockSpec((None, B, tn), lambda l, j: (l, 0, j)),
        out_shape=jax.ShapeDtypeStruct((L, B, N), F32),
        compiler_params=_cparams(("arbitrary", "arbitrary")),
        name="adaln",
    )(c.T, ada_w, ada_b)


def _even_in_kernel(x_ref, g_ref, sc_ref, sh_ref, w_ref, uf_ref, glu_ref):
    h = _rms_mod(x_ref[...], g_ref[...], sc_ref[...], sh_ref[...])
    p = jnp.dot(h.astype(BF16), w_ref[...], preferred_element_type=F32)
    fw, cw = FOURIER_WIDTH, CONV_WIDTH
    uf_ref[...] = p[:, :fw].astype(BF16)
    glu_ref[...] = p[:, fw:fw + cw] * jax.nn.sigmoid(p[:, fw + cw:])


def even_in(x, g, scale, shift, w, tm=512):
    B, S, D = x.shape
    N = w.shape[1]
    return pl.pallas_call(
        _even_in_kernel,
        grid=(B, S // tm),
        in_specs=[
            pl.BlockSpec((None, tm, D), lambda b, i: (b, i, 0)),
            pl.BlockSpec((1, D), lambda b, i: (0, 0)),
            pl.BlockSpec((None, 1, D), lambda b, i: (b, 0, 0)),
            pl.BlockSpec((None, 1, D), lambda b, i: (b, 0, 0)),
            pl.BlockSpec((D, N), lambda b, i: (0, 0)),
        ],
        out_specs=[
            pl.BlockSpec((None, tm, FOURIER_WIDTH), lambda b, i: (b, i, 0)),
            pl.BlockSpec((None, tm, CONV_WIDTH), lambda b, i: (b, i, 0)),
        ],
        out_shape=[
            jax.ShapeDtypeStruct((B, S, FOURIER_WIDTH), BF16),
            jax.ShapeDtypeStruct((B, S, CONV_WIDTH), F32),
        ],
        compiler_params=_cparams(("arbitrary", "arbitrary")),
        name="even_in",
    )(x, g, scale, shift, w)


def _fourier_consts(S):
    n2 = FFT_N2
    n1 = S // n2
    C = GROUP_CH
    scale = 1.0 / np.sqrt(float(S) * C)
    cm = 2.0 * np.pi * np.outer(np.arange(C), np.arange(C)) / C
    ccs = np.concatenate([np.cos(cm), -np.sin(cm)], axis=1) * scale
    ac = 2.0 * np.pi * np.outer(np.arange(n1), np.arange(n1)) / n1
    eye = np.eye(SUBLANES)
    kcat = np.concatenate([np.kron(np.cos(ac), eye), np.kron(-np.sin(ac), eye)], axis=0)
    cc = np.repeat(np.arange(n1), SUBLANES)[:, None]
    ii = np.tile(np.arange(SUBLANES), n1)[:, None]
    jj = np.arange(n2 // SUBLANES)[None, :]
    ang = 2.0 * np.pi * (SUBLANES * jj + ii) * cc / S
    twr, twi = np.cos(ang), -np.sin(ang)
    bd = 2.0 * np.pi * np.outer(np.arange(n2), np.arange(n2)) / n2
    w2 = np.concatenate([np.cos(bd), np.sin(bd)], axis=1)
    f = lambda a: jnp.asarray(a, F32)
    return f(ccs).astype(BF16), f(kcat).astype(BF16), f(twr), f(twi), f(w2).astype(BF16)


def _fourier_kernel(u_ref, ccs_ref, kcat_ref, twr_ref, twi_ref, w2_ref, o_ref, z_ref, y_ref, *, S):
    n2 = FFT_N2
    n1 = S // n2
    R = SUBLANES * n1
    C = GROUP_CH
    ch = min(S, 512)
    for r in range(S // ch):
        z_ref[r * ch:(r + 1) * ch, :] = jnp.dot(u_ref[r * ch:(r + 1) * ch, :], ccs_ref[...],
                                                preferred_element_type=F32)
    kcat = kcat_ref[...]
    for j in range(n2 // SUBLANES):
        r0 = SUBLANES * j
        d = jnp.concatenate([z_ref[a * n2 + r0:a * n2 + r0 + SUBLANES, :] for a in range(n1)], axis=0)
        p = jnp.dot(kcat, d.astype(BF16), preferred_element_type=F32)
        o_r = p[:R, :C] - p[R:, C:]
        o_i = p[R:, :C] + p[:R, C:]
        tr = twr_ref[:, j:j + 1]
        ti = twi_ref[:, j:j + 1]
        a_r = o_r * tr - o_i * ti
        a_i = o_r * ti + o_i * tr
        for c in range(n1):
            z_ref[c * n2 + r0:c * n2 + r0 + SUBLANES, :C] = a_r[SUBLANES * c:SUBLANES * (c + 1), :]
            z_ref[c * n2 + r0:c * n2 + r0 + SUBLANES, C:] = a_i[SUBLANES * c:SUBLANES * (c + 1), :]
    w2 = w2_ref[...]
    for c in range(n1):
        a = z_ref[c * n2:(c + 1) * n2, :]
        rhs = jnp.concatenate([a[:, :C], a[:, C:]], axis=0).astype(BF16)
        y_ref[pl.ds(c, n2, stride=n1), :] = jnp.dot(w2, rhs, preferred_element_type=F32)
    o_ref[...] = y_ref[...].astype(BF16)


def fourier_mix(uf):
    B, S, W = uf.shape
    consts = _fourier_consts(S)
    const_specs = [pl.BlockSpec(c.shape, lambda b, g: (0, 0)) for c in consts]
    return pl.pallas_call(
        functools.partial(_fourier_kernel, S=S),
        grid=(B, W // GROUP_CH),
        in_specs=[pl.BlockSpec((None, S, GROUP_CH), lambda b, g: (b, 0, g))] + const_specs,
        out_specs=pl.BlockSpec((None, S, GROUP_CH), lambda b, g: (b, 0, g)),
        out_shape=jax.ShapeDtypeStruct((B, S, W), BF16),
        scratch_shapes=[pltpu.VMEM((S, 2 * GROUP_CH), F32), pltpu.VMEM((S, GROUP_CH), F32)],
        compiler_params=_cparams(("arbitrary", "arbitrary")),
        name="fourier",
    )(uf, *consts)


def _conv_kernel(prev_ref, cur_ref, next_ref, w_ref, b_ref, g_ref, beta_ref, o_ref, ext_ref, *, tm, nt):
    i = pl.program_id(1)
    ext_ref[0:HALO, :] = jnp.where(i > 0, prev_ref[...], 0.0)
    ext_ref[HALO:HALO + tm, :] = cur_ref[...]
    ext_ref[HALO + tm:2 * HALO + tm, :] = jnp.where(i < nt - 1, next_ref[...], 0.0)
    pad = CONV_KERNEL // 2
    acc = ext_ref[HALO - pad:HALO - pad + tm, :] * w_ref[0:1, :]
    for t in range(1, CONV_KERNEL):
        o = HALO - pad + t
        acc = acc + ext_ref[o:o + tm, :] * w_ref[t:t + 1, :]
    conv = acc + b_ref[...]
    mu = jnp.mean(conv, axis=-1, keepdims=True)
    var = jnp.mean(jnp.square(conv - mu), axis=-1, keepdims=True)
    z = (conv - mu) * lax.rsqrt(var + EPS) * g_ref[...] + beta_ref[...]
    o_ref[...] = (z * jax.nn.sigmoid(z)).astype(BF16)


def conv_ln_swish(glu, w, b, g, beta, tm=512):
    B, S, C = glu.shape
    nt = S // tm
    hb = tm // HALO
    nh = S // HALO
    return pl.pallas_call(
        functools.partial(_conv_kernel, tm=tm, nt=nt),
        grid=(B, nt),
        in_specs=[
            pl.BlockSpec((None, HALO, C), lambda bb, i: (bb, jnp.maximum(i * hb - 1, 0), 0)),
            pl.BlockSpec((None, tm, C), lambda bb, i: (bb, i, 0)),
            pl.BlockSpec((None, HALO, C), lambda bb, i: (bb, jnp.minimum((i + 1) * hb, nh - 1), 0)),
            pl.BlockSpec((CONV_KERNEL, C), lambda bb, i: (0, 0)),
            pl.BlockSpec((1, C), lambda bb, i: (0, 0)),
            pl.BlockSpec((1, C), lambda bb, i: (0, 0)),
            pl.BlockSpec((1, C), lambda bb, i: (0, 0)),
        ],
        out_specs=pl.BlockSpec((None, tm, C), lambda bb, i: (bb, i, 0)),
        out_shape=jax.ShapeDtypeStruct((B, S, C), BF16),
        scratch_shapes=[pltpu.VMEM((tm + 2 * HALO, C), F32)],
        compiler_params=_cparams(("arbitrary", "arbitrary")),
        name="conv",
    )(glu, glu, glu, w, b, g, beta)


def _rope_tab_kernel(pos_ref, inv_ref, sgn_ref, cos_ref, sin_ref):
    ang = pos_ref[...].astype(F32) * inv_ref[...]
    cos_ref[...] = jnp.cos(ang)
    sin_ref[...] = jnp.sin(ang) * sgn_ref[...]


def rope_tables(positions, ts=1024):
    B, S = positions.shape
    half = HEAD_DIM // 2
    inv = ROPE_THETA ** (-jnp.arange(0, HEAD_DIM, 2, dtype=F32) / HEAD_DIM)
    reps = LANES // half
    inv_row = jnp.tile(inv, reps)[None, :]
    sgn_row = jnp.tile(jnp.concatenate([-jnp.ones((half,), F32), jnp.ones((half,), F32)]), reps // 2)[None, :]
    return pl.pallas_call(
        _rope_tab_kernel,
        grid=(B, S // ts),
        in_specs=[
            pl.BlockSpec((None, ts, 1), lambda b, i: (b, i, 0)),
            pl.BlockSpec((1, LANES), lambda b, i: (0, 0)),
            pl.BlockSpec((1, LANES), lambda b, i: (0, 0)),
        ],
        out_specs=[pl.BlockSpec((None, ts, LANES), lambda b, i: (b, i, 0))] * 2,
        out_shape=[jax.ShapeDtypeStruct((B, S, LANES), F32)] * 2,
        compiler_params=_cparams(("arbitrary", "arbitrary")),
        name="rope_tab",
    )(positions.reshape(B, S, 1), inv_row, sgn_row)


def _rope_block(t, cos, sin_signed, first_half):
    half = HEAD_DIM // 2
    rot = jnp.where(first_half, pltpu.roll(t, LANES - half, 1), pltpu.roll(t, half, 1))
    return t * cos + rot * sin_signed


def _odd_in_kernel(x_ref, g_ref, sc_ref, sh_ref, w_ref, cos_ref, sin_ref, q_ref, k_ref, v_ref):
    h = _rms_mod(x_ref[...], g_ref[...], sc_ref[...], sh_ref[...])
    p = jnp.dot(h.astype(BF16), w_ref[...], preferred_element_type=F32)
    qd = N_Q_HEADS * HEAD_DIM
    kd = N_KV_HEADS * HEAD_DIM
    cos = cos_ref[...]
    sin = sin_ref[...]
    lane = lax.broadcasted_iota(I32, cos.shape, 1)
    first_half = (lane % HEAD_DIM) < (HEAD_DIM // 2)
    qscale = HEAD_DIM ** -0.5
    for j in range(qd // LANES):
        blk = _rope_block(p[:, j * LANES:(j + 1) * LANES], cos, sin, first_half)
        q_ref[:, j * LANES:(j + 1) * LANES] = (blk * qscale).astype(BF16)
    for j in range(kd // LANES):
        blk = _rope_block(p[:, qd + j * LANES:qd + (j + 1) * LANES], cos, sin, first_half)
        k_ref[:, j * LANES:(j + 1) * LANES] = blk.astype(BF16)
    v_ref[...] = p[:, qd + kd:].astype(BF16)


def odd_in(x, g, scale, shift, w, cos, sin, tm=512):
    B, S, D = x.shape
    N = w.shape[1]
    qd = N_Q_HEADS * HEAD_DIM
    kd = N_KV_HEADS * HEAD_DIM
    return pl.pallas_call(
        _odd_in_kernel,
        grid=(B, S // tm),
        in_specs=[
            pl.BlockSpec((None, tm, D), lambda b, i: (b, i, 0)),
            pl.BlockSpec((1, D), lambda b, i: (0, 0)),
            pl.BlockSpec((None, 1, D), lambda b, i: (b, 0, 0)),
            pl.BlockSpec((None, 1, D), lambda b, i: (b, 0, 0)),
            pl.BlockSpec((D, N), lambda b, i: (0, 0)),
            pl.BlockSpec((None, tm, LANES), lambda b, i: (b, i, 0)),
            pl.BlockSpec((None, tm, LANES), lambda b, i: (b, i, 0)),
        ],
        out_specs=[
            pl.BlockSpec((None, tm, qd), lambda b, i: (b, i, 0)),
            pl.BlockSpec((None, tm, kd), lambda b, i: (b, i, 0)),
            pl.BlockSpec((None, tm, kd), lambda b, i: (b, i, 0)),
        ],
        out_shape=[
            jax.ShapeDtypeStruct((B, S, qd), BF16),
            jax.ShapeDtypeStruct((B, S, kd), BF16),
            jax.ShapeDtypeStruct((B, S, kd), BF16),
        ],
        compiler_params=_cparams(("arbitrary", "arbitrary")),
        name="odd_in",
    )(x, g, scale, shift, w, cos, sin)


def _attn_kernel(sink_ref, q_ref, kp_ref, kc_ref, kn_ref, vp_ref, vc_ref, vn_ref, o_ref, *, nb):
    n = pl.program_id(1)
    T = BLOCK
    kwin = jnp.concatenate([kp_ref[...], kc_ref[...], kn_ref[...]], axis=0)
    vwin = jnp.concatenate([vp_ref[...], vc_ref[...], vn_ref[...]], axis=0)
    qi = lax.broadcasted_iota(I32, (T, 3 * T), 0)
    kj = lax.broadcasted_iota(I32, (T, 3 * T), 1)
    rel = kj - T - qi
    kpos = (n - 1) * T + kj
    valid1 = (jnp.abs(rel) <= BLOCK) & (kpos >= 0) & (kpos < nb * T)
    valid = jnp.concatenate([valid1] * GQA, axis=0)
    grp = lax.broadcasted_iota(I32, (T, GQA * HEAD_DIM), 1) // HEAD_DIM
    for kh in range(N_KV_HEADS):
        kk = kwin[:, kh * HEAD_DIM:(kh + 1) * HEAD_DIM]
        vv = vwin[:, kh * HEAD_DIM:(kh + 1) * HEAD_DIM]
        k4 = jnp.concatenate([kk] * GQA, axis=1)
        v4 = jnp.concatenate([vv] * GQA, axis=1)
        qh = q_ref[:, kh * GQA * HEAD_DIM:(kh + 1) * GQA * HEAD_DIM]
        qm = jnp.concatenate([jnp.where(grp == g, qh, jnp.zeros_like(qh)) for g in range(GQA)], axis=0)
        s = lax.dot_general(qm, k4, (((1,), (1,)), ((), ())), preferred_element_type=F32)
        s = jnp.where(valid, s, -jnp.inf)
        sink = jnp.concatenate(
            [jnp.full((T, 1), sink_ref[kh * GQA + g], F32) for g in range(GQA)], axis=0)
        m = jnp.maximum(jnp.max(s, axis=-1, keepdims=True), sink)
        e = jnp.exp(s - m)
        p = e / (jnp.sum(e, axis=-1, keepdims=True) + jnp.exp(sink - m))
        o4 = jnp.dot(p.astype(BF16), v4, preferred_element_type=F32)
        out = jnp.where(grp == 0, o4[0:T], 0.0)
        for g in range(1, GQA):
            out = out + jnp.where(grp == g, o4[g * T:(g + 1) * T], 0.0)
        o_ref[:, kh * GQA * HEAD_DIM:(kh + 1) * GQA * HEAD_DIM] = out.astype(BF16)


def attention(q, k, v, sink):
    B, S, qd = q.shape
    kd = k.shape[-1]
    nb = S // BLOCK
    prev = lambda b, n: (b, jnp.maximum(n - 1, 0), 0)
    cur = lambda b, n: (b, n, 0)
    nxt = lambda b, n: (b, jnp.minimum(n + 1, nb - 1), 0)
    kv = lambda im: pl.BlockSpec((None, BLOCK, kd), im)
    return pl.pallas_call(
        functools.partial(_attn_kernel, nb=nb),
        grid=(B, nb),
        in_specs=[
            pl.BlockSpec(memory_space=pltpu.SMEM),
            pl.BlockSpec((None, BLOCK, qd), cur),
            kv(prev), kv(cur), kv(nxt), kv(prev), kv(cur), kv(nxt),
        ],
        out_specs=pl.BlockSpec((None, BLOCK, qd), cur),
        out_shape=jax.ShapeDtypeStruct((B, S, qd), BF16),
        compiler_params=_cparams(("arbitrary", "arbitrary")),
        name="attn",
    )(sink, q, k, k, k, v, v, v)


def _post_kernel(*refs, n_y):
    y_refs = refs[:n_y]
    w_refs = refs[n_y:2 * n_y]
    x_ref, gate_ref, g2_ref, sc2_ref, sh2_ref, rwh_ref, rwl_ref, rb_ref = refs[2 * n_y:2 * n_y + 8]
    xo_ref, h_ref, aff_ref = refs[2 * n_y + 8:]
    acc = jnp.dot(y_refs[0][...], w_refs[0][...], preferred_element_type=F32)
    for yr, wr in zip(y_refs[1:], w_refs[1:]):
        acc = acc + jnp.dot(yr[...], wr[...], preferred_element_type=F32)
    xn = x_ref[...] + gate_ref[...] * acc
    xo_ref[...] = xn
    h = _rms_mod(xn, g2_ref[...], sc2_ref[...], sh2_ref[...])
    h_ref[...] = h
    hh = h.astype(BF16)
    hl = (h - hh.astype(F32)).astype(BF16)
    rwh = rwh_ref[...]
    logits = (jnp.dot(hh, rwh, preferred_element_type=F32)
              + jnp.dot(hl, rwh, preferred_element_type=F32)
              + jnp.dot(hh, rwl_ref[...], preferred_element_type=F32)) + rb_ref[...]
    lane = lax.broadcasted_iota(I32, logits.shape, 1)
    logits = jnp.where(lane < N_EXPERTS, logits, -jnp.inf)
    m = jnp.max(logits, axis=-1, keepdims=True)
    e = jnp.exp(logits - m)
    aff = e / jnp.sum(e, axis=-1, keepdims=True)
    aff_ref[...] = aff.T[:N_EXPERTS, :]


def post_mixer(ys, ws, x, gate, g2, sc2, sh2, rwh, rwl, rb, tm=512):
    B, S, D = x.shape
    n_y = len(ys)
    in_specs = [pl.BlockSpec((None, tm, y.shape[-1]), lambda b, i: (b, i, 0)) for y in ys]
    in_specs += [pl.BlockSpec(w.shape, lambda b, i: (0, 0)) for w in ws]
    in_specs += [
        pl.BlockSpec((None, tm, D), lambda b, i: (b, i, 0)),
        pl.BlockSpec((None, 1, D), lambda b, i: (b, 0, 0)),
        pl.BlockSpec((1, D), lambda b, i: (0, 0)),
        pl.BlockSpec((None, 1, D), lambda b, i: (b, 0, 0)),
        pl.BlockSpec((None, 1, D), lambda b, i: (b, 0, 0)),
        pl.BlockSpec((D, LANES), lambda b, i: (0, 0)),
        pl.BlockSpec((D, LANES), lambda b, i: (0, 0)),
        pl.BlockSpec((1, LANES), lambda b, i: (0, 0)),
    ]
    return pl.pallas_call(
        functools.partial(_post_kernel, n_y=n_y),
        grid=(B, S // tm),
        in_specs=in_specs,
        out_specs=[
            pl.BlockSpec((None, tm, D), lambda b, i: (b, i, 0)),
            pl.BlockSpec((None, tm, D), lambda b, i: (b, i, 0)),
            pl.BlockSpec((None, N_EXPERTS, tm), lambda b, i: (b, 0, i)),
        ],
        out_shape=[
            jax.ShapeDtypeStruct((B, S, D), F32),
            jax.ShapeDtypeStruct((B, S, D), F32),
            jax.ShapeDtypeStruct((B, N_EXPERTS, S), F32),
        ],
        compiler_params=_cparams(("arbitrary", "arbitrary")),
        name="post",
    )(*ys, *ws, x, gate, g2, sc2, sh2, rwh, rwl, rb)


def _select_consts(S):
    nt = S // LANES
    E = N_EXPERTS
    li = np.arange(LANES)
    u_incl = (li[:, None] <= li[None, :]).astype(np.float32)
    u_strict = (li[:, None] < li[None, :]).astype(np.float32)
    r = np.arange(nt * E)
    tt, ee = r // E, r % E
    l_strict = ((ee[:, None] == ee[None, :]) & (tt[None, :] < tt[:, None])).astype(np.float32)
    tind = (np.arange(S)[:, None] // LANES == li[None, :]).astype(np.float32)
    f = lambda a: jnp.asarray(a, F32).astype(BF16)
    return f(u_incl), f(u_strict), f(l_strict), f(tind)


def _select_kernel(aff_ref, ui_ref, us_ref, ls_ref, tind_ref, idx_ref, gate_ref, ts_ref,
                   affx_ref, lcs_ref, cex_ref, cin_ref, *, S, cap):
    E = N_EXPERTS
    nt = S // LANES
    aff = aff_ref[...]
    v = pltpu.bitcast(aff, I32)
    capf = jnp.float32(cap)

    def count(mask):
        return jnp.sum(jnp.where(mask, 1.0, 0.0), axis=1, keepdims=True)

    def search(i, t):
        cand = t | jnp.left_shift(jnp.int32(1), 30 - i)
        return jnp.where(count(v >= cand) >= capf, cand, t)

    thr = lax.fori_loop(0, 31, search, jnp.zeros((E, 1), I32))
    gt = v > thr
    eq = v == thr
    need = capf - count(gt)

    def to_tiles(m):
        return jnp.concatenate([m[:, LANES * t:LANES * (t + 1)] for t in range(nt)], axis=0)

    def prefix(mt):
        lcs = jnp.dot(mt.astype(BF16), ui_ref[...], preferred_element_type=F32)
        off = jnp.dot(ls_ref[...], lcs.astype(BF16), preferred_element_type=F32)[:, LANES - 1:LANES]
        return lcs, off

    eqt = to_tiles(jnp.where(eq, 1.0, 0.0))
    lcs_e, off_e = prefix(eqt)
    rank_t = lcs_e + off_e - eqt
    rank = jnp.concatenate([rank_t[E * t:E * (t + 1), :] for t in range(nt)], axis=1)
    sel = gt | (eq & (rank < need))
    self_ = jnp.where(sel, 1.0, 0.0)

    tot_et = jnp.dot(self_.astype(BF16), tind_ref[...], preferred_element_type=F32)
    ts_ref[...] = jnp.dot(tot_et.astype(BF16), us_ref[...], preferred_element_type=F32).astype(I32)

    selt = to_tiles(self_)
    lcs, off = prefix(selt)
    nrow = nt * E
    zeros = jnp.zeros((nrow, LANES), F32)
    pad_rows = LANES * E - nrow
    lcs_ref[0:nrow, :] = lcs
    lcs_ref[nrow:LANES * E, :] = jnp.zeros((pad_rows, LANES), F32)
    affx_ref[0:nrow, :] = to_tiles(aff)
    affx_ref[nrow:LANES * E, :] = jnp.zeros((pad_rows, LANES), F32)
    cex_ref[...] = off + zeros
    cin_ref[...] = off + lcs[:, LANES - 1:LANES] + zeros

    jrow = lax.broadcasted_iota(I32, (1, cap), 1).astype(F32)
    trow = lax.broadcasted_iota(I32, (LANES, 1), 0).astype(F32)

    def per_expert(e, carry):
        lc = lcs_ref[pl.ds(e, LANES, stride=E), :]
        af = affx_ref[pl.ds(e, LANES, stride=E), :]
        cx = cex_ref[pl.ds(e, nt, stride=E), :][:, 0:1]
        ci = cin_ref[pl.ds(e, nt, stride=E), :][:, 0:1]
        le = jnp.where(ci <= jrow, 1.0, 0.0)
        tj = jnp.sum(le, axis=0, keepdims=True)
        r = jrow - jnp.sum(le * (ci - cx), axis=0, keepdims=True)
        onehot = jnp.where(trow == tj, 1.0, 0.0).astype(BF16)
        g = jnp.dot(lc.T.astype(BF16), onehot, preferred_element_type=F32)
        lo = jnp.sum(jnp.where(g <= r, 1.0, 0.0), axis=0, keepdims=True)
        idx_ref[pl.ds(e, 1), :] = (tj * LANES + lo).astype(I32)
        aft = af.T
        a1 = aft.astype(BF16)
        r1 = aft - a1.astype(F32)
        a2 = r1.astype(BF16)
        a3 = (r1 - a2.astype(F32)).astype(BF16)
        ag = (jnp.dot(a1, onehot, preferred_element_type=F32) + jnp.dot(a2, onehot, preferred_element_type=F32)
              + jnp.dot(a3, onehot, preferred_element_type=F32))
        gate_ref[pl.ds(e, 1), :] = jnp.sum(jnp.where(trow == lo, ag, 0.0), axis=0, keepdims=True)
        return carry

    lax.fori_loop(0, E, per_expert, 0)


def select_tokens(aff_t):
    B, E, S = aff_t.shape
    cap = CAPACITY_FACTOR * S // E
    nt = S // LANES
    consts = _select_consts(S)
    const_specs = [pl.BlockSpec(c.shape, lambda b: (0, 0)) for c in consts]
    return pl.pallas_call(
        functools.partial(_select_kernel, S=S, cap=cap),
        grid=(B,),
        in_specs=[pl.BlockSpec((None, E, S), lambda b: (b, 0, 0))] + const_specs,
        out_specs=[
            pl.BlockSpec((None, E, cap), lambda b: (b, 0, 0)),
            pl.BlockSpec((None, E, cap), lambda b: (b, 0, 0)),
            pl.BlockSpec((None, E, LANES), lambda b: (b, 0, 0)),
        ],
        out_shape=[
            jax.ShapeDtypeStruct((B, E, cap), I32),
            jax.ShapeDtypeStruct((B, E, cap), F32),
            jax.ShapeDtypeStruct((B, E, LANES), I32),
        ],
        scratch_shapes=[
            pltpu.VMEM((LANES * E, LANES), F32),
            pltpu.VMEM((LANES * E, LANES), F32),
            pltpu.VMEM((nt * E, LANES), F32),
            pltpu.VMEM((nt * E, LANES), F32),
        ],
        compiler_params=_cparams(("arbitrary",)),
        name="select",
    )(aff_t, *consts)


def _ffn_kernel(idx_ref, gate_ref, h_hbm, wg_ref, wu_ref, wd_ref, y_ref, xg_ref, sem, *, cap):
    b = pl.program_id(1)

    def row_copy(j, src_row):
        return pltpu.make_async_copy(h_hbm.at[b, pl.ds(src_row, 1), :], xg_ref.at[pl.ds(j, 1), :], sem)

    def issue(j, c):
        row_copy(j, idx_ref[0, j]).start()
        return c

    lax.fori_loop(0, cap, issue, 0)

    def wait(j, c):
        row_copy(j, 0).wait()
        return c

    lax.fori_loop(0, cap, wait, 0)
    xg = xg_ref[...].astype(BF16)
    a = jnp.dot(xg, wg_ref[...], preferred_element_type=F32)
    u = jnp.dot(xg, wu_ref[...], preferred_element_type=F32)
    hid = ((a * jax.nn.sigmoid(a)) * u).astype(BF16)
    y = jnp.dot(hid, wd_ref[...], preferred_element_type=F32)
    gcol = jnp.broadcast_to(gate_ref[...], (LANES, cap)).T[:, 0:1]
    y_ref[...] = y * gcol


def expert_ffn(idx, gate, h, wg, wu, wd):
    B, E, cap = idx.shape
    _, S, D = h.shape
    Fdim = wg.shape[-1]
    return pl.pallas_call(
        functools.partial(_ffn_kernel, cap=cap),
        grid=(E, B),
        in_specs=[
            pl.BlockSpec((None, 1, cap), lambda e, b: (b * E + e, 0, 0), memory_space=pltpu.SMEM),
            pl.BlockSpec((None, 1, cap), lambda e, b: (b * E + e, 0, 0)),
            pl.BlockSpec(memory_space=pl.ANY),
            pl.BlockSpec((None, D, Fdim), lambda e, b: (e, 0, 0)),
            pl.BlockSpec((None, D, Fdim), lambda e, b: (e, 0, 0)),
            pl.BlockSpec((None, Fdim, D), lambda e, b: (e, 0, 0)),
        ],
        out_specs=pl.BlockSpec((None, cap, D), lambda e, b: (b * E + e, 0, 0)),
        out_shape=jax.ShapeDtypeStruct((B * E, cap, D), F32),
        scratch_shapes=[pltpu.VMEM((cap, D), F32), pltpu.SemaphoreType.DMA],
        compiler_params=_cparams(("arbitrary", "arbitrary")),
        name="ffn",
    )(idx.reshape(B * E, 1, cap), gate.reshape(B * E, 1, cap), h, wg, wu, wd)


def _combine_kernel(idx_ref, ts_ref, x_ref, gate_ref, y_hbm, o_ref, ybuf_ref, sem, *, tc, cap, n_e):
    b = pl.program_id(0)
    c = pl.program_id(1)
    e = pl.program_id(2)
    tiles = tc // LANES
    lo = ts_ref[0, c * tiles]
    hi = ts_ref[0, (c + 1) * tiles]
    base = c * tc

    @pl.when(e == 0)
    def _():
        o_ref[...] = jnp.zeros_like(o_ref)

    k0 = lo // LANES
    k1 = (hi + LANES - 1) // LANES

    def blk_copy(k):
        return pltpu.make_async_copy(y_hbm.at[b * n_e + e, pl.ds(k * LANES, LANES), :], ybuf_ref, sem)

    def per_block(k, carry):
        cp = blk_copy(k)
        cp.start()
        cp.wait()
        j0 = jnp.maximum(lo, k * LANES)
        j1 = jnp.minimum(hi, (k + 1) * LANES)

        def per_row(j, cc):
            s = idx_ref[0, j] - base
            o_ref[pl.ds(s, 1), :] = o_ref[pl.ds(s, 1), :] + ybuf_ref[pl.ds(j - k * LANES, 1), :]
            return cc

        lax.fori_loop(j0, j1, per_row, 0)
        return carry

    lax.fori_loop(k0, jnp.where(hi > lo, k1, k0), per_block, 0)

    @pl.when(e == n_e - 1)
    def _():
        o_ref[...] = x_ref[...] + gate_ref[...] * o_ref[...]


def combine(idx, tstart, x, gate, y, tc=2048):
    B, S, D = x.shape
    _, E, cap = idx.shape
    tc = min(tc, S)
    return pl.pallas_call(
        functools.partial(_combine_kernel, tc=tc, cap=cap, n_e=E),
        grid=(B, S // tc, E),
        in_specs=[
            pl.BlockSpec((None, 1, cap), lambda b, c, e: (b * E + e, 0, 0), memory_space=pltpu.SMEM),
            pl.BlockSpec((None, 1, LANES), lambda b, c, e: (b * E + e, 0, 0), memory_space=pltpu.SMEM),
            pl.BlockSpec((None, tc, D), lambda b, c, e: (b, c, 0)),
            pl.BlockSpec((None, 1, D), lambda b, c, e: (b, 0, 0)),
            pl.BlockSpec(memory_space=pl.ANY),
        ],
        out_specs=pl.BlockSpec((None, tc, D), lambda b, c, e: (b, c, 0)),
        out_shape=jax.ShapeDtypeStruct((B, S, D), F32),
        scratch_shapes=[pltpu.VMEM((LANES, D), F32), pltpu.SemaphoreType.DMA],
        compiler_params=_cparams(("arbitrary", "arbitrary", "arbitrary")),
        name="combine",
    )(idx.reshape(B * E, 1, cap), tstart.reshape(B * E, 1, LANES), x, gate, y)


def _final_kernel(x_ref, g_ref, o_ref):
    x = x_ref[...]
    o_ref[...] = x * lax.rsqrt(jnp.mean(x * x, axis=-1, keepdims=True) + EPS) * g_ref[...]


def final_norm(x, g, tm=1024):
    B, S, D = x.shape
    return pl.pallas_call(
        _final_kernel,
        grid=(B, S // tm),
        in_specs=[pl.BlockSpec((None, tm, D), lambda b, i: (b, i, 0)), pl.BlockSpec((1, D), lambda b, i: (0, 0))],
        out_specs=pl.BlockSpec((None, tm, D), lambda b, i: (b, i, 0)),
        out_shape=jax.ShapeDtypeStruct((B, S, D), F32),
        compiler_params=_cparams(("arbitrary", "arbitrary")),
        name="final_norm",
    )(x, g)


def _pad_lanes(a):
    return jnp.pad(a, ((0, 0), (0, LANES - a.shape[-1])))


def moe_block(x, h, aff_t, gate, wg, wu, wd):
    idx, g, tstart = select_tokens(aff_t)
    y = expert_ffn(idx, g, h, wg, wu, wd)
    return combine(idx, tstart, x, gate, y)


def kernel(x, c, positions, ada_w, ada_b, mix_norm_g, ffn_norm_g, fc_w_in, conv_w, conv_b, conv_ln_g, conv_ln_b, fc_w_out, attn_w_qkv, attn_sink, attn_w_out, router_w, router_b, moe_w_gate, moe_w_up, moe_w_down, final_norm_g):
    B, S, D = x.shape
    depth = ada_w.shape[0]
    mod = adaln_all(c, ada_w.reshape(depth * 2, D, 3 * D), ada_b.reshape(depth * 2, 1, 3 * D))
    mod = mod.reshape(depth, 2, B, 1, 3 * D)
    cos, sin = rope_tables(positions)
    for l in range(depth):
        i = l // 2
        shift, scale, gate = (mod[l, 0, :, :, k * D:(k + 1) * D] for k in range(3))
        shift2, scale2, gate2 = (mod[l, 1, :, :, k * D:(k + 1) * D] for k in range(3))
        g1 = mix_norm_g[l][None, :]
        if l % 2 == 0:
            uf, glu = even_in(x, g1, scale, shift, fc_w_in[i].astype(BF16))
            yf = fourier_mix(uf)
            yc = conv_ln_swish(glu, conv_w[i], conv_b[i][None, :], conv_ln_g[i][None, :], conv_ln_b[i][None, :])
            wo = fc_w_out[i].astype(BF16)
            ys, ws = [yf, yc], [wo[:FOURIER_WIDTH], wo[FOURIER_WIDTH:]]
        else:
            q, k, v = odd_in(x, g1, scale, shift, attn_w_qkv[i].astype(BF16), cos, sin)
            ys, ws = [attention(q, k, v, attn_sink[i])], [attn_w_out[i].astype(BF16)]
        rw = _pad_lanes(router_w[l])
        rwh = rw.astype(BF16)
        rwl = (rw - rwh.astype(F32)).astype(BF16)
        x, h, aff_t = post_mixer(ys, ws, x, gate, ffn_norm_g[l][None, :], scale2, shift2,
                                 rwh, rwl, _pad_lanes(router_b[l][None, :]))
        x = moe_block(x, h, aff_t, gate2, moe_w_gate[l].astype(BF16), moe_w_up[l].astype(BF16),
                      moe_w_down[l].astype(BF16))
    return final_norm(x, final_norm_g[None, :])
```

```python
import functools

import numpy as np
import jax
import jax.numpy as jnp
from jax import lax
from jax.experimental import pallas as pl
from jax.experimental.pallas import tpu as pltpu

F32 = jnp.float32
BF16 = jnp.bfloat16
I32 = jnp.int32

EPS = 1e-6
HEAD_DIM = 64
N_Q_HEADS = 16
N_KV_HEADS = 4
GQA = N_Q_HEADS // N_KV_HEADS
BLOCK = 128
N_EXPERTS = 16
CAPACITY_FACTOR = 2
CONV_KERNEL = 31
FOURIER_WIDTH = 512
CONV_WIDTH = 512
GROUP_CH = 128
ROPE_THETA = 10000.0
LANES = 128
SUBLANES = 8
FFT_N2 = 256
HALO = 16
VMEM_LIMIT = 56 * 1024 * 1024


def _cparams(sem):
    return pltpu.CompilerParams(dimension_semantics=sem, vmem_limit_bytes=VMEM_LIMIT)


def _rms_mod(x, g, scale, shift):
    y = x * lax.rsqrt(jnp.mean(x * x, axis=-1, keepdims=True) + EPS)
    return (y * g) * (1.0 + scale) + shift


def _adaln_kernel(ct_ref, w_ref, b_ref, o_ref):
    ct = ct_ref[...]
    cond = ct * jax.nn.sigmoid(ct)
    w = w_ref[...]
    rows = [jnp.sum(w * cond[:, b:b + 1], axis=0, keepdims=True) for b in range(ct.shape[1])]
    o_ref[...] = jnp.concatenate(rows, axis=0) + b_ref[...]


def adaln_all(c, ada_w, ada_b):
    B, D = c.shape
    L, _, N = ada_w.shape
    tn = 1024
    return pl.pallas_call(
        _adaln_kernel,
        grid=(L, N // tn),
        in_specs=[
            pl.BlockSpec((D, B), lambda l, j: (0, 0)),
            pl.BlockSpec((None, D, tn), lambda l, j: (l, 0, j)),
            pl.BlockSpec((None, 1, tn), lambda l, j: (l, 0, j)),
        ],
        out_specs=pl.BlockSpec((None, B, tn), lambda l, j: (l, 0, j)),
        out_shape=jax.ShapeDtypeStruct((L, B, N), F32),
        compiler_params=_cparams(("arbitrary", "arbitrary")),
        name="adaln",
    )(c.T, ada_w, ada_b)


def _even_in_kernel(x_ref, g_ref, sc_ref, sh_ref, w_ref, uf_ref, glu_ref):
    h = _rms_mod(x_ref[...], g_ref[...], sc_ref[...], sh_ref[...])
    p = jnp.dot(h.astype(BF16), w_ref[...], preferred_element_type=F32)
    fw, cw = FOURIER_WIDTH, CONV_WIDTH
    uf_ref[...] = p[:, :fw].astype(BF16)
    glu_ref[...] = p[:, fw:fw + cw] * jax.nn.sigmoid(p[:, fw + cw:])


def even_in(x, g, scale, shift, w, tm=512):
    B, S, D = x.shape
    N = w.shape[1]
    return pl.pallas_call(
        _even_in_kernel,
        grid=(B, S // tm),
        in_specs=[
            pl.BlockSpec((None, tm, D), lambda b, i: (b, i, 0)),
            pl.BlockSpec((1, D), lambda b, i: (0, 0)),
            pl.BlockSpec((None, 1, D), lambda b, i: (b, 0, 0)),
            pl.BlockSpec((None, 1, D), lambda b, i: (b, 0, 0)),
            pl.BlockSpec((D, N), lambda b, i: (0, 0)),
        ],
        out_specs=[
            pl.BlockSpec((None, tm, FOURIER_WIDTH), lambda b, i: (b, i, 0)),
            pl.BlockSpec((None, tm, CONV_WIDTH), lambda b, i: (b, i, 0)),
        ],
        out_shape=[
            jax.ShapeDtypeStruct((B, S, FOURIER_WIDTH), BF16),
            jax.ShapeDtypeStruct((B, S, CONV_WIDTH), F32),
        ],
        compiler_params=_cparams(("arbitrary", "arbitrary")),
        name="even_in",
    )(x, g, scale, shift, w)


def _fourier_consts(S):
    n2 = FFT_N2
    n1 = S // n2
    C = GROUP_CH
    scale = 1.0 / np.sqrt(float(S) * C)
    cm = 2.0 * np.pi * np.outer(np.arange(C), np.arange(C)) / C
    ccs = np.concatenate([np.cos(cm), -np.sin(cm)], axis=1) * scale
    ac = 2.0 * np.pi * np.outer(np.arange(n1), np.arange(n1)) / n1
    eye = np.eye(SUBLANES)
    kcat = np.concatenate([np.kron(np.cos(ac), eye), np.kron(-np.sin(ac), eye)], axis=0)
    cc = np.repeat(np.arange(n1), SUBLANES)[:, None]
    ii = np.tile(np.arange(SUBLANES), n1)[:, None]
    jj = np.arange(n2 // SUBLANES)[None, :]
    ang = 2.0 * np.pi * (SUBLANES * jj + ii) * cc / S
    twr, twi = np.cos(ang), -np.sin(ang)
    bd = 2.0 * np.pi * np.outer(np.arange(n2), np.arange(n2)) / n2
    w2 = np.concatenate([np.cos(bd), np.sin(bd)], axis=1)
    f = lambda a: jnp.asarray(a, F32)
    return f(ccs).astype(BF16), f(kcat).astype(BF16), f(twr), f(twi), f(w2).astype(BF16)


def _fourier_kernel(u_ref, ccs_ref, kcat_ref, twr_ref, twi_ref, w2_ref, o_ref, z_ref, y_ref, *, S):
    n2 = FFT_N2
    n1 = S // n2
    R = SUBLANES * n1
    C = GROUP_CH
    ch = min(S, 512)
    for r in range(S // ch):
        z_ref[r * ch:(r + 1) * ch, :] = jnp.dot(u_ref[r * ch:(r + 1) * ch, :], ccs_ref[...],
                                                preferred_element_type=F32)
    kcat = kcat_ref[...]
    for j in range(n2 // SUBLANES):
        r0 = SUBLANES * j
        d = jnp.concatenate([z_ref[a * n2 + r0:a * n2 + r0 + SUBLANES, :] for a in range(n1)], axis=0)
        p = jnp.dot(kcat, d.astype(BF16), preferred_element_type=F32)
        o_r = p[:R, :C] - p[R:, C:]
        o_i = p[R:, :C] + p[:R, C:]
        tr = twr_ref[:, j:j + 1]
        ti = twi_ref[:, j:j + 1]
        a_r = o_r * tr - o_i * ti
        a_i = o_r * ti + o_i * tr
        for c in range(n1):
            z_ref[c * n2 + r0:c * n2 + r0 + SUBLANES, :C] = a_r[SUBLANES * c:SUBLANES * (c + 1), :]
            z_ref[c * n2 + r0:c * n2 + r0 + SUBLANES, C:] = a_i[SUBLANES * c:SUBLANES * (c + 1), :]
    w2 = w2_ref[...]
    for c in range(n1):
        a = z_ref[c * n2:(c + 1) * n2, :]
        rhs = jnp.concatenate([a[:, :C], a[:, C:]], axis=0).astype(BF16)
        y_ref[pl.ds(c, n2, stride=n1), :] = jnp.dot(w2, rhs, preferred_element_type=F32)
    o_ref[...] = y_ref[...].astype(BF16)


def fourier_mix(uf):
    B, S, W = uf.shape
    consts = _fourier_consts(S)
    const_specs = [pl.BlockSpec(c.shape, lambda b, g: (0, 0)) for c in consts]
    return pl.pallas_call(
        functools.partial(_fourier_kernel, S=S),
        grid=(B, W // GROUP_CH),
        in_specs=[pl.BlockSpec((None, S, GROUP_CH), lambda b, g: (b, 0, g))] + const_specs,
        out_specs=pl.BlockSpec((None, S, GROUP_CH), lambda b, g: (b, 0, g)),
        out_shape=jax.ShapeDtypeStruct((B, S, W), BF16),
        scratch_shapes=[pltpu.VMEM((S, 2 * GROUP_CH), F32), pltpu.VMEM((S, GROUP_CH), F32)],
        compiler_params=_cparams(("arbitrary", "arbitrary")),
        name="fourier",
    )(uf, *consts)


def _conv_kernel(prev_ref, cur_ref, next_ref, w_ref, b_ref, g_ref, beta_ref, o_ref, ext_ref, *, tm, nt):
    i = pl.program_id(1)
    ext_ref[0:HALO, :] = jnp.where(i > 0, prev_ref[...], 0.0)
    ext_ref[HALO:HALO + tm, :] = cur_ref[...]
    ext_ref[HALO + tm:2 * HALO + tm, :] = jnp.where(i < nt - 1, next_ref[...], 0.0)
    pad = CONV_KERNEL // 2
    acc = ext_ref[HALO - pad:HALO - pad + tm, :] * w_ref[0:1, :]
    for t in range(1, CONV_KERNEL):
        o = HALO - pad + t
        acc = acc + ext_ref[o:o + tm, :] * w_ref[t:t + 1, :]
    conv = acc + b_ref[...]
    mu = jnp.mean(conv, axis=-1, keepdims=True)
    var = jnp.mean(jnp.square(conv - mu), axis=-1, keepdims=True)
    z = (conv - mu) * lax.rsqrt(var + EPS) * g_ref[...] + beta_ref[...]
    o_ref[...] = (z * jax.nn.sigmoid(z)).astype(BF16)


def conv_ln_swish(glu, w, b, g, beta, tm=512):
    B, S, C = glu.shape
    nt = S // tm
    hb = tm // HALO
    nh = S // HALO
    return pl.pallas_call(
        functools.partial(_conv_kernel, tm=tm, nt=nt),
        grid=(B, nt),
        in_specs=[
            pl.BlockSpec((None, HALO, C), lambda bb, i: (bb, jnp.maximum(i * hb - 1, 0), 0)),
            pl.BlockSpec((None, tm, C), lambda bb, i: (bb, i, 0)),
            pl.BlockSpec((None, HALO, C), lambda bb, i: (bb, jnp.minimum((i + 1) * hb, nh - 1), 0)),
            pl.BlockSpec((CONV_KERNEL, C), lambda bb, i: (0, 0)),
            pl.BlockSpec((1, C), lambda bb, i: (0, 0)),
            pl.BlockSpec((1, C), lambda bb, i: (0, 0)),
            pl.BlockSpec((1, C), lambda bb, i: (0, 0)),
        ],
        out_specs=pl.BlockSpec((None, tm, C), lambda bb, i: (bb, i, 0)),
        out_shape=jax.ShapeDtypeStruct((B, S, C), BF16),
        scratch_shapes=[pltpu.VMEM((tm + 2 * HALO, C), F32)],
        compiler_params=_cparams(("arbitrary", "arbitrary")),
        name="conv",
    )(glu, glu, glu, w, b, g, beta)


def _rope_tab_kernel(pos_ref, inv_ref, sgn_ref, cos_ref, sin_ref):
    ang = pos_ref[...].astype(F32) * inv_ref[...]
    cos_ref[...] = jnp.cos(ang)
    sin_ref[...] = jnp.sin(ang) * sgn_ref[...]


def rope_tables(positions, ts=1024):
    B, S = positions.shape
    half = HEAD_DIM // 2
    inv = ROPE_THETA ** (-jnp.arange(0, HEAD_DIM, 2, dtype=F32) / HEAD_DIM)
    reps = LANES // half
    inv_row = jnp.tile(inv, reps)[None, :]
    sgn_row = jnp.tile(jnp.concatenate([-jnp.ones((half,), F32), jnp.ones((half,), F32)]), reps // 2)[None, :]
    return pl.pallas_call(
        _rope_tab_kernel,
        grid=(B, S // ts),
        in_specs=[
            pl.BlockSpec((None, ts, 1), lambda b, i: (b, i, 0)),
            pl.BlockSpec((1, LANES), lambda b, i: (0, 0)),
            pl.BlockSpec((1, LANES), lambda b, i: (0, 0)),
        ],
        out_specs=[pl.BlockSpec((None, ts, LANES), lambda b, i: (b, i, 0))] * 2,
        out_shape=[jax.ShapeDtypeStruct((B, S, LANES), F32)] * 2,
        compiler_params=_cparams(("arbitrary", "arbitrary")),
        name="rope_tab",
    )(positions.reshape(B, S, 1), inv_row, sgn_row)


def _rope_block(t, cos, sin_signed, first_half):
    half = HEAD_DIM // 2
    rot = jnp.where(first_half, pltpu.roll(t, LANES - half, 1), pltpu.roll(t, half, 1))
    return t * cos + rot * sin_signed


def _odd_in_kernel(x_ref, g_ref, sc_ref, sh_ref, w_ref, cos_ref, sin_ref, q_ref, k_ref, v_ref):
    h = _rms_mod(x_ref[...], g_ref[...], sc_ref[...], sh_ref[...])
    p = jnp.dot(h.astype(BF16), w_ref[...], preferred_element_type=F32)
    qd = N_Q_HEADS * HEAD_DIM
    kd = N_KV_HEADS * HEAD_DIM
    cos = cos_ref[...]
    sin = sin_ref[...]
    lane = lax.broadcasted_iota(I32, cos.shape, 1)
    first_half = (lane % HEAD_DIM) < (HEAD_DIM // 2)
    qscale = HEAD_DIM ** -0.5
    for j in range(qd // LANES):
        blk = _rope_block(p[:, j * LANES:(j + 1) * LANES], cos, sin, first_half)
        q_ref[:, j * LANES:(j + 1) * LANES] = (blk * qscale).astype(BF16)
    for j in range(kd // LANES):
        blk = _rope_block(p[:, qd + j * LANES:qd + (j + 1) * LANES], cos, sin, first_half)
        k_ref[:, j * LANES:(j + 1) * LANES] = blk.astype(BF16)
    v_ref[...] = p[:, qd + kd:].astype(BF16)


def odd_in(x, g, scale, shift, w, cos, sin, tm=512):
    B, S, D = x.shape
    N = w.shape[1]
    qd = N_Q_HEADS * HEAD_DIM
    kd = N_KV_HEADS * HEAD_DIM
    return pl.pallas_call(
        _odd_in_kernel,
        grid=(B, S // tm),
        in_specs=[
            pl.BlockSpec((None, tm, D), lambda b, i: (b, i, 0)),
            pl.BlockSpec((1, D), lambda b, i: (0, 0)),
            pl.BlockSpec((None, 1, D), lambda b, i: (b, 0, 0)),
            pl.BlockSpec((None, 1, D), lambda b, i: (b, 0, 0)),
            pl.BlockSpec((D, N), lambda b, i: (0, 0)),
            pl.BlockSpec((None, tm, LANES), lambda b, i: (b, i, 0)),
            pl.BlockSpec((None, tm, LANES), lambda b, i: (b, i, 0)),
        ],
        out_specs=[
            pl.BlockSpec((None, tm, qd), lambda b, i: (b, i, 0)),
            pl.BlockSpec((None, tm, kd), lambda b, i: (b, i, 0)),
            pl.BlockSpec((None, tm, kd), lambda b, i: (b, i, 0)),
        ],
        out_shape=[
            jax.ShapeDtypeStruct((B, S, qd), BF16),
            jax.ShapeDtypeStruct((B, S, kd), BF16),
            jax.ShapeDtypeStruct((B, S, kd), BF16),
        ],
        compiler_params=_cparams(("arbitrary", "arbitrary")),
        name="odd_in",
    )(x, g, scale, shift, w, cos, sin)


def _attn_kernel(sink_ref, q_ref, kp_ref, kc_ref, kn_ref, vp_ref, vc_ref, vn_ref, o_ref, *, nb):
    n = pl.program_id(1)
    T = BLOCK
    kwin = jnp.concatenate([kp_ref[...], kc_ref[...], kn_ref[...]], axis=0)
    vwin = jnp.concatenate([vp_ref[...], vc_ref[...], vn_ref[...]], axis=0)
    qi = lax.broadcasted_iota(I32, (T, 3 * T), 0)
    kj = lax.broadcasted_iota(I32, (T, 3 * T), 1)
    rel = kj - T - qi
    kpos = (n - 1) * T + kj
    valid1 = (jnp.abs(rel) <= BLOCK) & (kpos >= 0) & (kpos < nb * T)
    valid = jnp.concatenate([valid1] * GQA, axis=0)
    grp = lax.broadcasted_iota(I32, (T, GQA * HEAD_DIM), 1) // HEAD_DIM
    for kh in range(N_KV_HEADS):
        kk = kwin[:, kh * HEAD_DIM:(kh + 1) * HEAD_DIM]
        vv = vwin[:, kh * HEAD_DIM:(kh + 1) * HEAD_DIM]
        k4 = jnp.concatenate([kk] * GQA, axis=1)
        v4 = jnp.concatenate([vv] * GQA, axis=1)
        qh = q_ref[:, kh * GQA * HEAD_DIM:(kh + 1) * GQA * HEAD_DIM]
        qm = jnp.concatenate([jnp.where(grp == g, qh, jnp.zeros_like(qh)) for g in range(GQA)], axis=0)
        s = lax.dot_general(qm, k4, (((1,), (1,)), ((), ())), preferred_element_type=F32)
        s = jnp.where(valid, s, -jnp.inf)
        sink = jnp.concatenate(
            [jnp.full((T, 1), sink_ref[kh * GQA + g], F32) for g in range(GQA)], axis=0)
        m = jnp.maximum(jnp.max(s, axis=-1, keepdims=True), sink)
        e = jnp.exp(s - m)
        p = e / (jnp.sum(e, axis=-1, keepdims=True) + jnp.exp(sink - m))
        o4 = jnp.dot(p.astype(BF16), v4, preferred_element_type=F32)
        out = jnp.where(grp == 0, o4[0:T], 0.0)
        for g in range(1, GQA):
            out = out + jnp.where(grp == g, o4[g * T:(g + 1) * T], 0.0)
        o_ref[:, kh * GQA * HEAD_DIM:(kh + 1) * GQA * HEAD_DIM] = out.astype(BF16)


def attention(q, k, v, sink):
    B, S, qd = q.shape
    kd = k.shape[-1]
    nb = S // BLOCK
    prev = lambda b, n: (b, jnp.maximum(n - 1, 0), 0)
    cur = lambda b, n: (b, n, 0)
    nxt = lambda b, n: (b, jnp.minimum(n + 1, nb - 1), 0)
    kv = lambda im: pl.BlockSpec((None, BLOCK, kd), im)
    return pl.pallas_call(
        functools.partial(_attn_kernel, nb=nb),
        grid=(B, nb),
        in_specs=[
            pl.BlockSpec(memory_space=pltpu.SMEM),
            pl.BlockSpec((None, BLOCK, qd), cur),
            kv(prev), kv(cur), kv(nxt), kv(prev), kv(cur), kv(nxt),
        ],
        out_specs=pl.BlockSpec((None, BLOCK, qd), cur),
        out_shape=jax.ShapeDtypeStruct((B, S, qd), BF16),
        compiler_params=_cparams(("arbitrary", "arbitrary")),
        name="attn",
    )(sink, q, k, k, k, v, v, v)


def _post_kernel(*refs, n_y):
    y_refs = refs[:n_y]
    w_refs = refs[n_y:2 * n_y]
    x_ref, gate_ref, g2_ref, sc2_ref, sh2_ref, rwh_ref, rwl_ref, rb_ref = refs[2 * n_y:2 * n_y + 8]
    xo_ref, h_ref, aff_ref = refs[2 * n_y + 8:]
    acc = jnp.dot(y_refs[0][...], w_refs[0][...], preferred_element_type=F32)
    for yr, wr in zip(y_refs[1:], w_refs[1:]):
        acc = acc + jnp.dot(yr[...], wr[...], preferred_element_type=F32)
    xn = x_ref[...] + gate_ref[...] * acc
    xo_ref[...] = xn
    h = _rms_mod(xn, g2_ref[...], sc2_ref[...], sh2_ref[...])
    hh = h.astype(BF16)
    half = h.shape[1] // 2
    lo = pltpu.bitcast(hh[:, :half].astype(F32), jnp.uint32) >> 16
    hi = pltpu.bitcast(hh[:, half:].astype(F32), jnp.uint32) & jnp.uint32(0xFFFF0000)
    h_ref[...] = lo | hi
    hl = (h - hh.astype(F32)).astype(BF16)
    rwh = rwh_ref[...]
    logits = (jnp.dot(hh, rwh, preferred_element_type=F32)
              + jnp.dot(hl, rwh, preferred_element_type=F32)
              + jnp.dot(hh, rwl_ref[...], preferred_element_type=F32)) + rb_ref[...]
    lane = lax.broadcasted_iota(I32, logits.shape, 1)
    logits = jnp.where(lane < N_EXPERTS, logits, -jnp.inf)
    m = jnp.max(logits, axis=-1, keepdims=True)
    e = jnp.exp(logits - m)
    aff = e / jnp.sum(e, axis=-1, keepdims=True)
    aff_ref[...] = aff.T[:N_EXPERTS, :]


def post_mixer(ys, ws, x, gate, g2, sc2, sh2, rwh, rwl, rb, tm=512):
    B, S, D = x.shape
    n_y = len(ys)
    in_specs = [pl.BlockSpec((None, tm, y.shape[-1]), lambda b, i: (b, i, 0)) for y in ys]
    in_specs += [pl.BlockSpec(w.shape, lambda b, i: (0, 0)) for w in ws]
    in_specs += [
        pl.BlockSpec((None, tm, D), lambda b, i: (b, i, 0)),
        pl.BlockSpec((None, 1, D), lambda b, i: (b, 0, 0)),
        pl.BlockSpec((1, D), lambda b, i: (0, 0)),
        pl.BlockSpec((None, 1, D), lambda b, i: (b, 0, 0)),
        pl.BlockSpec((None, 1, D), lambda b, i: (b, 0, 0)),
        pl.BlockSpec((D, LANES), lambda b, i: (0, 0)),
        pl.BlockSpec((D, LANES), lambda b, i: (0, 0)),
        pl.BlockSpec((1, LANES), lambda b, i: (0, 0)),
    ]
    return pl.pallas_call(
        functools.partial(_post_kernel, n_y=n_y),
        grid=(B, S // tm),
        in_specs=in_specs,
        out_specs=[
            pl.BlockSpec((None, tm, D), lambda b, i: (b, i, 0)),
            pl.BlockSpec((None, tm, D // 2), lambda b, i: (b, i, 0)),
            pl.BlockSpec((None, N_EXPERTS, tm), lambda b, i: (b, 0, i)),
        ],
        out_shape=[
            jax.ShapeDtypeStruct((B, S, D), F32),
            jax.ShapeDtypeStruct((B, S, D // 2), jnp.uint32),
            jax.ShapeDtypeStruct((B, N_EXPERTS, S), F32),
        ],
        compiler_params=_cparams(("arbitrary", "arbitrary")),
        name="post",
    )(*ys, *ws, x, gate, g2, sc2, sh2, rwh, rwl, rb)


def _select_consts(S):
    nt = S // LANES
    E = N_EXPERTS
    li = np.arange(LANES)
    u_incl = (li[:, None] <= li[None, :]).astype(np.float32)
    u_strict = (li[:, None] < li[None, :]).astype(np.float32)
    r = np.arange(nt * E)
    tt, ee = r // E, r % E
    l_strict = ((ee[:, None] == ee[None, :]) & (tt[None, :] < tt[:, None])).astype(np.float32)
    tind = (np.arange(S)[:, None] // LANES == li[None, :]).astype(np.float32)
    f = lambda a: jnp.asarray(a, F32).astype(BF16)
    return f(u_incl), f(u_strict), f(l_strict), f(tind)


def _select_kernel(aff_ref, ui_ref, us_ref, ls_ref, tind_ref, idx_ref, gate_ref, ts_ref,
                   affx_ref, lcs_ref, cex_ref, cin_ref, *, S, cap):
    E = N_EXPERTS
    nt = S // LANES
    aff = aff_ref[...]
    v = pltpu.bitcast(aff, I32)
    capf = jnp.float32(cap)

    def count(mask):
        return jnp.sum(jnp.where(mask, 1.0, 0.0), axis=1, keepdims=True)

    def search(i, t):
        cand = t | jnp.left_shift(jnp.int32(1), 30 - i)
        return jnp.where(count(v >= cand) >= capf, cand, t)

    thr = lax.fori_loop(0, 31, search, jnp.zeros((E, 1), I32))
    gt = v > thr
    eq = v == thr
    need = capf - count(gt)

    def to_tiles(m):
        return jnp.concatenate([m[:, LANES * t:LANES * (t + 1)] for t in range(nt)], axis=0)

    def prefix(mt):
        lcs = jnp.dot(mt.astype(BF16), ui_ref[...], preferred_element_type=F32)
        off = jnp.dot(ls_ref[...], lcs.astype(BF16), preferred_element_type=F32)[:, LANES - 1:LANES]
        return lcs, off

    eqt = to_tiles(jnp.where(eq, 1.0, 0.0))
    lcs_e, off_e = prefix(eqt)
    rank_t = lcs_e + off_e - eqt
    rank = jnp.concatenate([rank_t[E * t:E * (t + 1), :] for t in range(nt)], axis=1)
    sel = gt | (eq & (rank < need))
    self_ = jnp.where(sel, 1.0, 0.0)

    tot_et = jnp.dot(self_.astype(BF16), tind_ref[...], preferred_element_type=F32)
    ts_ref[...] = jnp.dot(tot_et.astype(BF16), us_ref[...], preferred_element_type=F32).astype(I32)

    selt = to_tiles(self_)
    lcs, off = prefix(selt)
    nrow = nt * E
    zeros = jnp.zeros((nrow, LANES), F32)
    pad_rows = LANES * E - nrow
    lcs_ref[0:nrow, :] = lcs
    lcs_ref[nrow:LANES * E, :] = jnp.zeros((pad_rows, LANES), F32)
    affx_ref[0:nrow, :] = to_tiles(aff)
    affx_ref[nrow:LANES * E, :] = jnp.zeros((pad_rows, LANES), F32)
    cex_ref[...] = off + zeros
    cin_ref[...] = off + lcs[:, LANES - 1:LANES] + zeros

    jrow = lax.broadcasted_iota(I32, (1, cap), 1).astype(F32)
    trow = lax.broadcasted_iota(I32, (LANES, 1), 0).astype(F32)

    def per_expert(e, carry):
        lc = lcs_ref[pl.ds(e, LANES, stride=E), :]
        af = affx_ref[pl.ds(e, LANES, stride=E), :]
        cx = cex_ref[pl.ds(e, nt, stride=E), :][:, 0:1]
        ci = cin_ref[pl.ds(e, nt, stride=E), :][:, 0:1]
        le = jnp.where(ci <= jrow, 1.0, 0.0)
        tj = jnp.sum(le, axis=0, keepdims=True)
        r = jrow - jnp.sum(le * (ci - cx), axis=0, keepdims=True)
        onehot = jnp.where(trow == tj, 1.0, 0.0).astype(BF16)
        g = jnp.dot(lc.T.astype(BF16), onehot, preferred_element_type=F32)
        lo = jnp.sum(jnp.where(g <= r, 1.0, 0.0), axis=0, keepdims=True)
        idx_ref[pl.ds(e, 1), :] = (tj * LANES + lo).astype(I32)
        aft = af.T
        a1 = aft.astype(BF16)
        r1 = aft - a1.astype(F32)
        a2 = r1.astype(BF16)
        a3 = (r1 - a2.astype(F32)).astype(BF16)
        ag = (jnp.dot(a1, onehot, preferred_element_type=F32) + jnp.dot(a2, onehot, preferred_element_type=F32)
              + jnp.dot(a3, onehot, preferred_element_type=F32))
        gate_ref[pl.ds(e, 1), :] = jnp.sum(jnp.where(trow == lo, ag, 0.0), axis=0, keepdims=True)
        return carry

    lax.fori_loop(0, E, per_expert, 0)


def select_tokens(aff_t):
    B, E, S = aff_t.shape
    cap = CAPACITY_FACTOR * S // E
    nt = S // LANES
    consts = _select_consts(S)
    const_specs = [pl.BlockSpec(c.shape, lambda b: (0, 0)) for c in consts]
    return pl.pallas_call(
        functools.partial(_select_kernel, S=S, cap=cap),
        grid=(B,),
        in_specs=[pl.BlockSpec((None, E, S), lambda b: (b, 0, 0))] + const_specs,
        out_specs=[
            pl.BlockSpec((None, E, cap), lambda b: (b, 0, 0)),
            pl.BlockSpec((None, E, cap), lambda b: (b, 0, 0)),
            pl.BlockSpec((None, E, LANES), lambda b: (b, 0, 0)),
        ],
        out_shape=[
            jax.ShapeDtypeStruct((B, E, cap), I32),
            jax.ShapeDtypeStruct((B, E, cap), F32),
            jax.ShapeDtypeStruct((B, E, LANES), I32),
        ],
        scratch_shapes=[
            pltpu.VMEM((LANES * E, LANES), F32),
            pltpu.VMEM((LANES * E, LANES), F32),
            pltpu.VMEM((nt * E, LANES), F32),
            pltpu.VMEM((nt * E, LANES), F32),
        ],
        compiler_params=_cparams(("arbitrary",)),
        name="select",
    )(aff_t, *consts)


GATHER_UNROLL = 8
FFN_ROWS = 256


def _ffn_kernel(idx_ref, gate_ref, hp_ref, wg_ref, wu_ref, wd_ref, y_ref, xg_ref, *, cap):
    def gather(t, c):
        j0 = t * GATHER_UNROLL
        rows = [hp_ref[pl.ds(idx_ref[0, j0 + r], 1), :] for r in range(GATHER_UNROLL)]
        for r in range(GATHER_UNROLL):
            xg_ref[pl.ds(j0 + r, 1), :] = rows[r]
        return c

    lax.fori_loop(0, cap // GATHER_UNROLL, gather, 0)
    half = wg_ref.shape[0] // 2
    gcol = jnp.broadcast_to(gate_ref[...], (LANES, cap)).T[:, 0:1]
    rows = min(FFN_ROWS, cap)
    for r in range(cap // rows):
        sl = slice(r * rows, (r + 1) * rows)
        xp = xg_ref[sl, :]
        lo = pltpu.bitcast(xp << 16, F32).astype(BF16)
        hi = pltpu.bitcast(xp & jnp.uint32(0xFFFF0000), F32).astype(BF16)
        a = (jnp.dot(lo, wg_ref[:half, :], preferred_element_type=F32)
             + jnp.dot(hi, wg_ref[half:, :], preferred_element_type=F32))
        u = (jnp.dot(lo, wu_ref[:half, :], preferred_element_type=F32)
             + jnp.dot(hi, wu_ref[half:, :], preferred_element_type=F32))
        hid = ((a * jax.nn.sigmoid(a)) * u).astype(BF16)
        y = jnp.dot(hid, wd_ref[...], preferred_element_type=F32)
        y_ref[sl, :] = y * gcol[sl, :]


def expert_ffn(idx, gate, hp, wg, wu, wd):
    B, E, cap = idx.shape
    _, S, Dh = hp.shape
    D = 2 * Dh
    Fdim = wg.shape[-1]
    return pl.pallas_call(
        functools.partial(_ffn_kernel, cap=cap),
        grid=(B, E),
        in_specs=[
            pl.BlockSpec((None, 1, cap), lambda b, e: (b * E + e, 0, 0), memory_space=pltpu.SMEM),
            pl.BlockSpec((None, 1, cap), lambda b, e: (b * E + e, 0, 0)),
            pl.BlockSpec((None, S, Dh), lambda b, e: (b, 0, 0), pipeline_mode=pl.Buffered(1)),
            pl.BlockSpec((None, D, Fdim), lambda b, e: (e, 0, 0)),
            pl.BlockSpec((None, D, Fdim), lambda b, e: (e, 0, 0)),
            pl.BlockSpec((None, Fdim, D), lambda b, e: (e, 0, 0)),
        ],
        out_specs=pl.BlockSpec((None, cap, D), lambda b, e: (b * E + e, 0, 0)),
        out_shape=jax.ShapeDtypeStruct((B * E, cap, D), F32),
        scratch_shapes=[pltpu.VMEM((cap, Dh), jnp.uint32)],
        compiler_params=_cparams(("arbitrary", "arbitrary")),
        name="ffn",
    )(idx.reshape(B * E, 1, cap), gate.reshape(B * E, 1, cap), hp, wg, wu, wd)


COMBINE_EG = 4
COMBINE_BR = 128
COMBINE_NBUF = 2
COMBINE_UNROLL = 4


def _combine_kernel(idx_ref, ts_ref, x_ref, gate_ref, y_hbm, o_ref, ybuf_ref, sem, be_ref, bk_ref,
                    *, tc, cap, n_e):
    b = pl.program_id(0)
    c = pl.program_id(1)
    g = pl.program_id(2)
    eg, br = COMBINE_EG, COMBINE_BR
    tiles = tc // LANES
    base = c * tc

    @pl.when(g == 0)
    def _():
        o_ref[...] = jnp.zeros_like(o_ref)

    def slot_range(el):
        return ts_ref[el, c * tiles], ts_ref[el, (c + 1) * tiles]

    def build(el, n):
        lo, hi = slot_range(el)
        k0 = lo // br
        k1 = jnp.where(hi > lo, (hi + br - 1) // br, k0)

        def push(k, m):
            be_ref[m] = el
            bk_ref[m] = k
            return m + 1

        return lax.fori_loop(k0, k1, push, n)

    nblk = lax.fori_loop(0, eg, build, 0)

    def blk_copy(i, slot):
        row = b * n_e + g * eg + be_ref[i]
        return pltpu.make_async_copy(y_hbm.at[row, pl.ds(bk_ref[i] * br, br), :], ybuf_ref.at[slot], sem.at[slot])

    @pl.when(nblk > 0)
    def _():
        blk_copy(0, 0).start()

    def per_block(i, carry):
        slot = i % COMBINE_NBUF

        @pl.when(i + 1 < nblk)
        def _():
            blk_copy(i + 1, (i + 1) % COMBINE_NBUF).start()

        blk_copy(i, slot).wait()
        el = be_ref[i]
        r0 = bk_ref[i] * br
        lo, hi = slot_range(el)
        j0 = jnp.maximum(lo, r0)
        j1 = jnp.minimum(hi, r0 + br)

        def add_rows(j, n):
            toks = [idx_ref[el, j + r] - base for r in range(n)]
            vals = [o_ref[pl.ds(toks[r], 1), :] + ybuf_ref[slot, pl.ds(j + r - r0, 1), :] for r in range(n)]
            for r in range(n):
                o_ref[pl.ds(toks[r], 1), :] = vals[r]

        ngrp = (j1 - j0) // COMBINE_UNROLL

        def grp(t, cc):
            add_rows(j0 + t * COMBINE_UNROLL, COMBINE_UNROLL)
            return cc

        lax.fori_loop(0, ngrp, grp, 0)

        def one(j, cc):
            add_rows(j, 1)
            return cc

        lax.fori_loop(j0 + ngrp * COMBINE_UNROLL, j1, one, 0)
        return carry

    lax.fori_loop(0, nblk, per_block, 0)

    @pl.when(g == n_e // eg - 1)
    def _():
        o_ref[...] = x_ref[...] + gate_ref[...] * o_ref[...]


def combine(idx, tstart, x, gate, y, tc=2048):
    B, S, D = x.shape
    _, E, cap = idx.shape
    tc = min(tc, S)
    eg = COMBINE_EG
    ng = E // eg
    max_blocks = eg * (cap // COMBINE_BR + 1)
    return pl.pallas_call(
        functools.partial(_combine_kernel, tc=tc, cap=cap, n_e=E),
        grid=(B, S // tc, ng),
        in_specs=[
            pl.BlockSpec((None, eg, cap), lambda b, c, g: (b * ng + g, 0, 0), memory_space=pltpu.SMEM),
            pl.BlockSpec((None, eg, LANES), lambda b, c, g: (b * ng + g, 0, 0), memory_space=pltpu.SMEM),
            pl.BlockSpec((None, tc, D), lambda b, c, g: (b, c, 0)),
            pl.BlockSpec((None, 1, D), lambda b, c, g: (b, 0, 0)),
            pl.BlockSpec(memory_space=pl.ANY),
        ],
        out_specs=pl.BlockSpec((None, tc, D), lambda b, c, g: (b, c, 0)),
        out_shape=jax.ShapeDtypeStruct((B, S, D), F32),
        scratch_shapes=[
            pltpu.VMEM((COMBINE_NBUF, COMBINE_BR, D), F32),
            pltpu.SemaphoreType.DMA((COMBINE_NBUF,)),
            pltpu.SMEM((max_blocks,), I32),
            pltpu.SMEM((max_blocks,), I32),
        ],
        compiler_params=_cparams(("arbitrary", "arbitrary", "arbitrary")),
        name="combine",
    )(idx.reshape(B * ng, eg, cap), tstart.reshape(B * ng, eg, LANES), x, gate, y)


def _final_kernel(x_ref, g_ref, o_ref):
    x = x_ref[...]
    o_ref[...] = x * lax.rsqrt(jnp.mean(x * x, axis=-1, keepdims=True) + EPS) * g_ref[...]


def final_norm(x, g, tm=1024):
    B, S, D = x.shape
    return pl.pallas_call(
        _final_kernel,
        grid=(B, S // tm),
        in_specs=[pl.BlockSpec((None, tm, D), lambda b, i: (b, i, 0)), pl.BlockSpec((1, D), lambda b, i: (0, 0))],
        out_specs=pl.BlockSpec((None, tm, D), lambda b, i: (b, i, 0)),
        out_shape=jax.ShapeDtypeStruct((B, S, D), F32),
        compiler_params=_cparams(("arbitrary", "arbitrary")),
        name="final_norm",
    )(x, g)


def _pad_lanes(a):
    return jnp.pad(a, ((0, 0), (0, LANES - a.shape[-1])))


def moe_block(x, h, aff_t, gate, wg, wu, wd):
    idx, g, tstart = select_tokens(aff_t)
    y = expert_ffn(idx, g, h, wg, wu, wd)
    return combine(idx, tstart, x, gate, y)


def kernel(x, c, positions, ada_w, ada_b, mix_norm_g, ffn_norm_g, fc_w_in, conv_w, conv_b, conv_ln_g, conv_ln_b, fc_w_out, attn_w_qkv, attn_sink, attn_w_out, router_w, router_b, moe_w_gate, moe_w_up, moe_w_down, final_norm_g):
    B, S, D = x.shape
    depth = ada_w.shape[0]
    mod = adaln_all(c, ada_w.reshape(depth * 2, D, 3 * D), ada_b.reshape(depth * 2, 1, 3 * D))
    mod = mod.reshape(depth, 2, B, 1, 3 * D)
    cos, sin = rope_tables(positions)
    for l in range(depth):
        i = l // 2
        shift, scale, gate = (mod[l, 0, :, :, k * D:(k + 1) * D] for k in range(3))
        shift2, scale2, gate2 = (mod[l, 1, :, :, k * D:(k + 1) * D] for k in range(3))
        g1 = mix_norm_g[l][None, :]
        if l % 2 == 0:
            uf, glu = even_in(x, g1, scale, shift, fc_w_in[i].astype(BF16))
            yf = fourier_mix(uf)
            yc = conv_ln_swish(glu, conv_w[i], conv_b[i][None, :], conv_ln_g[i][None, :], conv_ln_b[i][None, :])
            wo = fc_w_out[i].astype(BF16)
            ys, ws = [yf, yc], [wo[:FOURIER_WIDTH], wo[FOURIER_WIDTH:]]
        else:
            q, k, v = odd_in(x, g1, scale, shift, attn_w_qkv[i].astype(BF16), cos, sin)
            ys, ws = [attention(q, k, v, attn_sink[i])], [attn_w_out[i].astype(BF16)]
        rw = _pad_lanes(router_w[l])
        rwh = rw.astype(BF16)
        rwl = (rw - rwh.astype(F32)).astype(BF16)
        x, h, aff_t = post_mixer(ys, ws, x, gate, ffn_norm_g[l][None, :], scale2, shift2,
                                 rwh, rwl, _pad_lanes(router_b[l][None, :]))
        x = moe_block(x, h, aff_t, gate2, moe_w_gate[l].astype(BF16), moe_w_up[l].astype(BF16),
                      moe_w_down[l].astype(BF16))
    return final_norm(x, final_norm_g[None, :])
```

```python
import functools

import numpy as np
import jax
import jax.numpy as jnp
from jax import lax
from jax.experimental import pallas as pl
from jax.experimental.pallas import tpu as pltpu

F32 = jnp.float32
BF16 = jnp.bfloat16
I32 = jnp.int32

EPS = 1e-6
HEAD_DIM = 64
N_Q_HEADS = 16
N_KV_HEADS = 4
GQA = N_Q_HEADS // N_KV_HEADS
BLOCK = 128
N_EXPERTS = 16
CAPACITY_FACTOR = 2
CONV_KERNEL = 31
FOURIER_WIDTH = 512
CONV_WIDTH = 512
GROUP_CH = 128
ROPE_THETA = 10000.0
LANES = 128
SUBLANES = 8
FFT_N2 = 256
HALO = 16
VMEM_LIMIT = 56 * 1024 * 1024


def _cparams(sem):
    return pltpu.CompilerParams(dimension_semantics=sem, vmem_limit_bytes=VMEM_LIMIT)


def _rms_mod(x, g, scale, shift):
    y = x * lax.rsqrt(jnp.mean(x * x, axis=-1, keepdims=True) + EPS)
    return (y * g) * (1.0 + scale) + shift


def _adaln_kernel(ct_ref, w_ref, b_ref, o_ref):
    ct = ct_ref[...]
    cond = ct * jax.nn.sigmoid(ct)
    w = w_ref[...]
    rows = [jnp.sum(w * cond[:, b:b + 1], axis=0, keepdims=True) for b in range(ct.shape[1])]
    o_ref[...] = jnp.concatenate(rows, axis=0) + b_ref[...]


def adaln_all(c, ada_w, ada_b):
    B, D = c.shape
    L, _, N = ada_w.shape
    tn = 1024
    return pl.pallas_call(
        _adaln_kernel,
        grid=(L, N // tn),
        in_specs=[
            pl.BlockSpec((D, B), lambda l, j: (0, 0)),
            pl.BlockSpec((None, D, tn), lambda l, j: (l, 0, j)),
            pl.BlockSpec((None, 1, tn), lambda l, j: (l, 0, j)),
        ],
        out_specs=pl.BlockSpec((None, B, tn), lambda l, j: (l, 0, j)),
        out_shape=jax.ShapeDtypeStruct((L, B, N), F32),
        compiler_params=_cparams(("arbitrary", "arbitrary")),
        name="adaln",
    )(c.T, ada_w, ada_b)


def _even_in_kernel(x_ref, g_ref, sc_ref, sh_ref, w_ref, uf_ref, glu_ref):
    h = _rms_mod(x_ref[...], g_ref[...], sc_ref[...], sh_ref[...])
    p = jnp.dot(h.astype(BF16), w_ref[...], preferred_element_type=F32)
    fw, cw = FOURIER_WIDTH, CONV_WIDTH
    uf_ref[...] = p[:, :fw].astype(BF16)
    glu_ref[...] = p[:, fw:fw + cw] * jax.nn.sigmoid(p[:, fw + cw:])


def even_in(x, g, scale, shift, w, tm=512):
    B, S, D = x.shape
    N = w.shape[1]
    return pl.pallas_call(
        _even_in_kernel,
        grid=(B, S // tm),
        in_specs=[
            pl.BlockSpec((None, tm, D), lambda b, i: (b, i, 0)),
            pl.BlockSpec((1, D), lambda b, i: (0, 0)),
            pl.BlockSpec((None, 1, D), lambda b, i: (b, 0, 0)),
            pl.BlockSpec((None, 1, D), lambda b, i: (b, 0, 0)),
            pl.BlockSpec((D, N), lambda b, i: (0, 0)),
        ],
        out_specs=[
            pl.BlockSpec((None, tm, FOURIER_WIDTH), lambda b, i: (b, i, 0)),
            pl.BlockSpec((None, tm, CONV_WIDTH), lambda b, i: (b, i, 0)),
        ],
        out_shape=[
            jax.ShapeDtypeStruct((B, S, FOURIER_WIDTH), BF16),
            jax.ShapeDtypeStruct((B, S, CONV_WIDTH), F32),
        ],
        compiler_params=_cparams(("arbitrary", "arbitrary")),
        name="even_in",
    )(x, g, scale, shift, w)


def _fourier_consts(S):
    n2 = FFT_N2
    n1 = S // n2
    C = GROUP_CH
    scale = 1.0 / np.sqrt(float(S) * C)
    cm = 2.0 * np.pi * np.outer(np.arange(C), np.arange(C)) / C
    ccs = np.concatenate([np.cos(cm), -np.sin(cm)], axis=1) * scale
    ac = 2.0 * np.pi * np.outer(np.arange(n1), np.arange(n1)) / n1
    eye = np.eye(SUBLANES)
    kcat = np.concatenate([np.kron(np.cos(ac), eye), np.kron(-np.sin(ac), eye)], axis=0)
    cc = np.repeat(np.arange(n1), SUBLANES)[:, None]
    ii = np.tile(np.arange(SUBLANES), n1)[:, None]
    jj = np.arange(n2 // SUBLANES)[None, :]
    ang = 2.0 * np.pi * (SUBLANES * jj + ii) * cc / S
    twr, twi = np.cos(ang), -np.sin(ang)
    bd = 2.0 * np.pi * np.outer(np.arange(n2), np.arange(n2)) / n2
    w2 = np.concatenate([np.cos(bd), np.sin(bd)], axis=1)
    f = lambda a: jnp.asarray(a, F32)
    return f(ccs).astype(BF16), f(kcat).astype(BF16), f(twr), f(twi), f(w2).astype(BF16)


def _fourier_kernel(u_ref, ccs_ref, kcat_ref, twr_ref, twi_ref, w2_ref, o_ref, z_ref, y_ref, *, S):
    n2 = FFT_N2
    n1 = S // n2
    R = SUBLANES * n1
    C = GROUP_CH
    ch = min(S, 512)
    for r in range(S // ch):
        z_ref[r * ch:(r + 1) * ch, :] = jnp.dot(u_ref[r * ch:(r + 1) * ch, :], ccs_ref[...],
                                                preferred_element_type=F32)
    kcat = kcat_ref[...]
    for j in range(n2 // SUBLANES):
        r0 = SUBLANES * j
        d = jnp.concatenate([z_ref[a * n2 + r0:a * n2 + r0 + SUBLANES, :] for a in range(n1)], axis=0)
        p = jnp.dot(kcat, d.astype(BF16), preferred_element_type=F32)
        o_r = p[:R, :C] - p[R:, C:]
        o_i = p[R:, :C] + p[:R, C:]
        tr = twr_ref[:, j:j + 1]
        ti = twi_ref[:, j:j + 1]
        a_r = o_r * tr - o_i * ti
        a_i = o_r * ti + o_i * tr
        for c in range(n1):
            z_ref[c * n2 + r0:c * n2 + r0 + SUBLANES, :C] = a_r[SUBLANES * c:SUBLANES * (c + 1), :]
            z_ref[c * n2 + r0:c * n2 + r0 + SUBLANES, C:] = a_i[SUBLANES * c:SUBLANES * (c + 1), :]
    w2 = w2_ref[...]
    for c in range(n1):
        a = z_ref[c * n2:(c + 1) * n2, :]
        rhs = jnp.concatenate([a[:, :C], a[:, C:]], axis=0).astype(BF16)
        y_ref[pl.ds(c, n2, stride=n1), :] = jnp.dot(w2, rhs, preferred_element_type=F32)
    o_ref[...] = y_ref[...].astype(BF16)


def fourier_mix(uf):
    B, S, W = uf.shape
    consts = _fourier_consts(S)
    const_specs = [pl.BlockSpec(c.shape, lambda b, g: (0, 0)) for c in consts]
    return pl.pallas_call(
        functools.partial(_fourier_kernel, S=S),
        grid=(B, W // GROUP_CH),
        in_specs=[pl.BlockSpec((None, S, GROUP_CH), lambda b, g: (b, 0, g))] + const_specs,
        out_specs=pl.BlockSpec((None, S, GROUP_CH), lambda b, g: (b, 0, g)),
        out_shape=jax.ShapeDtypeStruct((B, S, W), BF16),
        scratch_shapes=[pltpu.VMEM((S, 2 * GROUP_CH), F32), pltpu.VMEM((S, GROUP_CH), F32)],
        compiler_params=_cparams(("arbitrary", "arbitrary")),
        name="fourier",
    )(uf, *consts)


def _conv_kernel(prev_ref, cur_ref, next_ref, w_ref, b_ref, g_ref, beta_ref, o_ref, ext_ref, *, tm, nt):
    i = pl.program_id(1)
    ext_ref[0:HALO, :] = jnp.where(i > 0, prev_ref[...], 0.0)
    ext_ref[HALO:HALO + tm, :] = cur_ref[...]
    ext_ref[HALO + tm:2 * HALO + tm, :] = jnp.where(i < nt - 1, next_ref[...], 0.0)
    pad = CONV_KERNEL // 2
    acc = ext_ref[HALO - pad:HALO - pad + tm, :] * w_ref[0:1, :]
    for t in range(1, CONV_KERNEL):
        o = HALO - pad + t
        acc = acc + ext_ref[o:o + tm, :] * w_ref[t:t + 1, :]
    conv = acc + b_ref[...]
    mu = jnp.mean(conv, axis=-1, keepdims=True)
    var = jnp.mean(jnp.square(conv - mu), axis=-1, keepdims=True)
    z = (conv - mu) * lax.rsqrt(var + EPS) * g_ref[...] + beta_ref[...]
    o_ref[...] = (z * jax.nn.sigmoid(z)).astype(BF16)


def conv_ln_swish(glu, w, b, g, beta, tm=512):
    B, S, C = glu.shape
    nt = S // tm
    hb = tm // HALO
    nh = S // HALO
    return pl.pallas_call(
        functools.partial(_conv_kernel, tm=tm, nt=nt),
        grid=(B, nt),
        in_specs=[
            pl.BlockSpec((None, HALO, C), lambda bb, i: (bb, jnp.maximum(i * hb - 1, 0), 0)),
            pl.BlockSpec((None, tm, C), lambda bb, i: (bb, i, 0)),
            pl.BlockSpec((None, HALO, C), lambda bb, i: (bb, jnp.minimum((i + 1) * hb, nh - 1), 0)),
            pl.BlockSpec((CONV_KERNEL, C), lambda bb, i: (0, 0)),
            pl.BlockSpec((1, C), lambda bb, i: (0, 0)),
            pl.BlockSpec((1, C), lambda bb, i: (0, 0)),
            pl.BlockSpec((1, C), lambda bb, i: (0, 0)),
        ],
        out_specs=pl.BlockSpec((None, tm, C), lambda bb, i: (bb, i, 0)),
        out_shape=jax.ShapeDtypeStruct((B, S, C), BF16),
        scratch_shapes=[pltpu.VMEM((tm + 2 * HALO, C), F32)],
        compiler_params=_cparams(("arbitrary", "arbitrary")),
        name="conv",
    )(glu, glu, glu, w, b, g, beta)


def _rope_tab_kernel(pos_ref, inv_ref, sgn_ref, cos_ref, sin_ref):
    ang = pos_ref[...].astype(F32) * inv_ref[...]
    cos_ref[...] = jnp.cos(ang)
    sin_ref[...] = jnp.sin(ang) * sgn_ref[...]


def rope_tables(positions, ts=1024):
    B, S = positions.shape
    half = HEAD_DIM // 2
    inv = ROPE_THETA ** (-jnp.arange(0, HEAD_DIM, 2, dtype=F32) / HEAD_DIM)
    reps = LANES // half
    inv_row = jnp.tile(inv, reps)[None, :]
    sgn_row = jnp.tile(jnp.concatenate([-jnp.ones((half,), F32), jnp.ones((half,), F32)]), reps // 2)[None, :]
    return pl.pallas_call(
        _rope_tab_kernel,
        grid=(B, S // ts),
        in_specs=[
            pl.BlockSpec((None, ts, 1), lambda b, i: (b, i, 0)),
            pl.BlockSpec((1, LANES), lambda b, i: (0, 0)),
            pl.BlockSpec((1, LANES), lambda b, i: (0, 0)),
        ],
        out_specs=[pl.BlockSpec((None, ts, LANES), lambda b, i: (b, i, 0))] * 2,
        out_shape=[jax.ShapeDtypeStruct((B, S, LANES), F32)] * 2,
        compiler_params=_cparams(("arbitrary", "arbitrary")),
        name="rope_tab",
    )(positions.reshape(B, S, 1), inv_row, sgn_row)


def _rope_block(t, cos, sin_signed, first_half):
    half = HEAD_DIM // 2
    rot = jnp.where(first_half, pltpu.roll(t, LANES - half, 1), pltpu.roll(t, half, 1))
    return t * cos + rot * sin_signed


def _odd_in_kernel(x_ref, g_ref, sc_ref, sh_ref, w_ref, cos_ref, sin_ref, q_ref, k_ref, v_ref):
    h = _rms_mod(x_ref[...], g_ref[...], sc_ref[...], sh_ref[...])
    p = jnp.dot(h.astype(BF16), w_ref[...], preferred_element_type=F32)
    qd = N_Q_HEADS * HEAD_DIM
    kd = N_KV_HEADS * HEAD_DIM
    cos = cos_ref[...]
    sin = sin_ref[...]
    lane = lax.broadcasted_iota(I32, cos.shape, 1)
    first_half = (lane % HEAD_DIM) < (HEAD_DIM // 2)
    qscale = HEAD_DIM ** -0.5
    for j in range(qd // LANES):
        blk = _rope_block(p[:, j * LANES:(j + 1) * LANES], cos, sin, first_half)
        q_ref[:, j * LANES:(j + 1) * LANES] = (blk * qscale).astype(BF16)
    for j in range(kd // LANES):
        blk = _rope_block(p[:, qd + j * LANES:qd + (j + 1) * LANES], cos, sin, first_half)
        k_ref[:, j * LANES:(j + 1) * LANES] = blk.astype(BF16)
    v_ref[...] = p[:, qd + kd:].astype(BF16)


def odd_in(x, g, scale, shift, w, cos, sin, tm=512):
    B, S, D = x.shape
    N = w.shape[1]
    qd = N_Q_HEADS * HEAD_DIM
    kd = N_KV_HEADS * HEAD_DIM
    return pl.pallas_call(
        _odd_in_kernel,
        grid=(B, S // tm),
        in_specs=[
            pl.BlockSpec((None, tm, D), lambda b, i: (b, i, 0)),
            pl.BlockSpec((1, D), lambda b, i: (0, 0)),
            pl.BlockSpec((None, 1, D), lambda b, i: (b, 0, 0)),
            pl.BlockSpec((None, 1, D), lambda b, i: (b, 0, 0)),
            pl.BlockSpec((D, N), lambda b, i: (0, 0)),
            pl.BlockSpec((None, tm, LANES), lambda b, i: (b, i, 0)),
            pl.BlockSpec((None, tm, LANES), lambda b, i: (b, i, 0)),
        ],
        out_specs=[
            pl.BlockSpec((None, tm, qd), lambda b, i: (b, i, 0)),
            pl.BlockSpec((None, tm, kd), lambda b, i: (b, i, 0)),
            pl.BlockSpec((None, tm, kd), lambda b, i: (b, i, 0)),
        ],
        out_shape=[
            jax.ShapeDtypeStruct((B, S, qd), BF16),
            jax.ShapeDtypeStruct((B, S, kd), BF16),
            jax.ShapeDtypeStruct((B, S, kd), BF16),
        ],
        compiler_params=_cparams(("arbitrary", "arbitrary")),
        name="odd_in",
    )(x, g, scale, shift, w, cos, sin)


def _attn_kernel(sink_ref, q_ref, kp_ref, kc_ref, kn_ref, vp_ref, vc_ref, vn_ref, o_ref, *, nb):
    n = pl.program_id(1)
    T = BLOCK
    kwin = jnp.concatenate([kp_ref[...], kc_ref[...], kn_ref[...]], axis=0)
    vwin = jnp.concatenate([vp_ref[...], vc_ref[...], vn_ref[...]], axis=0)
    qi = lax.broadcasted_iota(I32, (T, 3 * T), 0)
    kj = lax.broadcasted_iota(I32, (T, 3 * T), 1)
    rel = kj - T - qi
    kpos = (n - 1) * T + kj
    valid1 = (jnp.abs(rel) <= BLOCK) & (kpos >= 0) & (kpos < nb * T)
    valid = jnp.concatenate([valid1] * GQA, axis=0)
    grp = lax.broadcasted_iota(I32, (T, GQA * HEAD_DIM), 1) // HEAD_DIM
    left = lax.broadcasted_iota(I32, (T, LANES), 1) < HEAD_DIM
    ones = jnp.ones((3 * T, LANES), BF16)
    for kh in range(N_KV_HEADS):
        kk = kwin[:, kh * HEAD_DIM:(kh + 1) * HEAD_DIM]
        vv = vwin[:, kh * HEAD_DIM:(kh + 1) * HEAD_DIM]
        k4 = jnp.concatenate([kk] * GQA, axis=1)
        v1 = jnp.concatenate([vv, vv, ones], axis=1)
        qh = q_ref[:, kh * GQA * HEAD_DIM:(kh + 1) * GQA * HEAD_DIM]
        qm = jnp.concatenate([jnp.where(grp == g, qh, jnp.zeros_like(qh)) for g in range(GQA)], axis=0)
        s = lax.dot_general(qm, k4, (((1,), (1,)), ((), ())), preferred_element_type=F32)
        s = jnp.where(valid, s, -jnp.inf)
        sink = jnp.concatenate(
            [jnp.full((T, 1), sink_ref[kh * GQA + g], F32) for g in range(GQA)], axis=0)
        m = jnp.maximum(jnp.max(s, axis=-1, keepdims=True), sink)
        e = jnp.exp(s - m).astype(BF16)
        od = jnp.dot(e, v1, preferred_element_type=F32)
        o = od[:, :LANES] / (od[:, LANES:] + jnp.exp(sink - m))
        for t in range(GQA // 2):
            tile = jnp.where(left, o[2 * t * T:(2 * t + 1) * T], o[(2 * t + 1) * T:(2 * t + 2) * T])
            c0 = kh * GQA * HEAD_DIM + t * LANES
            o_ref[:, c0:c0 + LANES] = tile.astype(BF16)


def attention(q, k, v, sink):
    B, S, qd = q.shape
    kd = k.shape[-1]
    nb = S // BLOCK
    prev = lambda b, n: (b, jnp.maximum(n - 1, 0), 0)
    cur = lambda b, n: (b, n, 0)
    nxt = lambda b, n: (b, jnp.minimum(n + 1, nb - 1), 0)
    kv = lambda im: pl.BlockSpec((None, BLOCK, kd), im)
    return pl.pallas_call(
        functools.partial(_attn_kernel, nb=nb),
        grid=(B, nb),
        in_specs=[
            pl.BlockSpec(memory_space=pltpu.SMEM),
            pl.BlockSpec((None, BLOCK, qd), cur),
            kv(prev), kv(cur), kv(nxt), kv(prev), kv(cur), kv(nxt),
        ],
        out_specs=pl.BlockSpec((None, BLOCK, qd), cur),
        out_shape=jax.ShapeDtypeStruct((B, S, qd), BF16),
        compiler_params=_cparams(("arbitrary", "arbitrary")),
        name="attn",
    )(sink, q, k, k, k, v, v, v)


def _post_kernel(*refs, n_y):
    y_refs = refs[:n_y]
    w_refs = refs[n_y:2 * n_y]
    x_ref, gate_ref, g2_ref, sc2_ref, sh2_ref, rwh_ref, rwl_ref, rb_ref = refs[2 * n_y:2 * n_y + 8]
    xo_ref, h_ref, aff_ref = refs[2 * n_y + 8:]
    acc = jnp.dot(y_refs[0][...], w_refs[0][...], preferred_element_type=F32)
    for yr, wr in zip(y_refs[1:], w_refs[1:]):
        acc = acc + jnp.dot(yr[...], wr[...], preferred_element_type=F32)
    xn = x_ref[...] + gate_ref[...] * acc
    xo_ref[...] = xn
    h = _rms_mod(xn, g2_ref[...], sc2_ref[...], sh2_ref[...])
    hh = h.astype(BF16)
    half = h.shape[1] // 2
    lo = pltpu.bitcast(hh[:, :half].astype(F32), jnp.uint32) >> 16
    hi = pltpu.bitcast(hh[:, half:].astype(F32), jnp.uint32) & jnp.uint32(0xFFFF0000)
    h_ref[...] = lo | hi
    hl = (h - hh.astype(F32)).astype(BF16)
    rwh = rwh_ref[...]
    logits = (jnp.dot(hh, rwh, preferred_element_type=F32)
              + jnp.dot(hl, rwh, preferred_element_type=F32)
              + jnp.dot(hh, rwl_ref[...], preferred_element_type=F32)) + rb_ref[...]
    lane = lax.broadcasted_iota(I32, logits.shape, 1)
    logits = jnp.where(lane < N_EXPERTS, logits, -jnp.inf)
    m = jnp.max(logits, axis=-1, keepdims=True)
    e = jnp.exp(logits - m)
    aff = e / jnp.sum(e, axis=-1, keepdims=True)
    aff_ref[...] = aff.T[:N_EXPERTS, :]


def post_mixer(ys, ws, x, gate, g2, sc2, sh2, rwh, rwl, rb, tm=512):
    B, S, D = x.shape
    n_y = len(ys)
    in_specs = [pl.BlockSpec((None, tm, y.shape[-1]), lambda b, i: (b, i, 0)) for y in ys]
    in_specs += [pl.BlockSpec(w.shape, lambda b, i: (0, 0)) for w in ws]
    in_specs += [
        pl.BlockSpec((None, tm, D), lambda b, i: (b, i, 0)),
        pl.BlockSpec((None, 1, D), lambda b, i: (b, 0, 0)),
        pl.BlockSpec((1, D), lambda b, i: (0, 0)),
        pl.BlockSpec((None, 1, D), lambda b, i: (b, 0, 0)),
        pl.BlockSpec((None, 1, D), lambda b, i: (b, 0, 0)),
        pl.BlockSpec((D, LANES), lambda b, i: (0, 0)),
        pl.BlockSpec((D, LANES), lambda b, i: (0, 0)),
        pl.BlockSpec((1, LANES), lambda b, i: (0, 0)),
    ]
    return pl.pallas_call(
        functools.partial(_post_kernel, n_y=n_y),
        grid=(B, S // tm),
        in_specs=in_specs,
        out_specs=[
            pl.BlockSpec((None, tm, D), lambda b, i: (b, i, 0)),
            pl.BlockSpec((None, tm, D // 2), lambda b, i: (b, i, 0)),
            pl.BlockSpec((None, N_EXPERTS, tm), lambda b, i: (b, 0, i)),
        ],
        out_shape=[
            jax.ShapeDtypeStruct((B, S, D), F32),
            jax.ShapeDtypeStruct((B, S, D // 2), jnp.uint32),
            jax.ShapeDtypeStruct((B, N_EXPERTS, S), F32),
        ],
        compiler_params=_cparams(("arbitrary", "arbitrary")),
        name="post",
    )(*ys, *ws, x, gate, g2, sc2, sh2, rwh, rwl, rb)


def _select_consts(S):
    nt = S // LANES
    E = N_EXPERTS
    li = np.arange(LANES)
    u_incl = (li[:, None] <= li[None, :]).astype(np.float32)
    u_strict = (li[:, None] < li[None, :]).astype(np.float32)
    r = np.arange(nt * E)
    tt, ee = r // E, r % E
    l_strict = ((ee[:, None] == ee[None, :]) & (tt[None, :] < tt[:, None])).astype(np.float32)
    tind = (np.arange(S)[:, None] // LANES == li[None, :]).astype(np.float32)
    f = lambda a: jnp.asarray(a, F32).astype(BF16)
    return f(u_incl), f(u_strict), f(l_strict), f(tind)


def _select_kernel(aff_ref, ui_ref, us_ref, ls_ref, tind_ref, idx_ref, gate_ref, ts_ref,
                   affx_ref, lcs_ref, cex_ref, cin_ref, *, S, cap):
    E = N_EXPERTS
    nt = S // LANES
    aff = aff_ref[...]
    v = pltpu.bitcast(aff, I32)
    capf = jnp.float32(cap)

    def count(mask):
        return jnp.sum(jnp.where(mask, 1.0, 0.0), axis=1, keepdims=True)

    def search(i, t):
        cand = t | jnp.left_shift(jnp.int32(1), 30 - i)
        return jnp.where(count(v >= cand) >= capf, cand, t)

    thr = lax.fori_loop(0, 31, search, jnp.zeros((E, 1), I32))
    gt = v > thr
    eq = v == thr
    need = capf - count(gt)

    def to_tiles(m):
        return jnp.concatenate([m[:, LANES * t:LANES * (t + 1)] for t in range(nt)], axis=0)

    def prefix(mt):
        lcs = jnp.dot(mt.astype(BF16), ui_ref[...], preferred_element_type=F32)
        off = jnp.dot(ls_ref[...], lcs.astype(BF16), preferred_element_type=F32)[:, LANES - 1:LANES]
        return lcs, off

    eqt = to_tiles(jnp.where(eq, 1.0, 0.0))
    lcs_e, off_e = prefix(eqt)
    rank_t = lcs_e + off_e - eqt
    rank = jnp.concatenate([rank_t[E * t:E * (t + 1), :] for t in range(nt)], axis=1)
    sel = gt | (eq & (rank < need))
    self_ = jnp.where(sel, 1.0, 0.0)

    tot_et = jnp.dot(self_.astype(BF16), tind_ref[...], preferred_element_type=F32)
    ts_ref[...] = jnp.dot(tot_et.astype(BF16), us_ref[...], preferred_element_type=F32).astype(I32)

    selt = to_tiles(self_)
    lcs, off = prefix(selt)
    nrow = nt * E
    zeros = jnp.zeros((nrow, LANES), F32)
    pad_rows = LANES * E - nrow
    lcs_ref[0:nrow, :] = lcs
    lcs_ref[nrow:LANES * E, :] = jnp.zeros((pad_rows, LANES), F32)
    affx_ref[0:nrow, :] = to_tiles(aff)
    affx_ref[nrow:LANES * E, :] = jnp.zeros((pad_rows, LANES), F32)
    cex_ref[...] = off + zeros
    cin_ref[...] = off + lcs[:, LANES - 1:LANES] + zeros

    jrow = lax.broadcasted_iota(I32, (1, cap), 1).astype(F32)
    trow = lax.broadcasted_iota(I32, (LANES, 1), 0).astype(F32)

    def per_expert(e, carry):
        lc = lcs_ref[pl.ds(e, LANES, stride=E), :]
        af = affx_ref[pl.ds(e, LANES, stride=E), :]
        cx = cex_ref[pl.ds(e, nt, stride=E), :][:, 0:1]
        ci = cin_ref[pl.ds(e, nt, stride=E), :][:, 0:1]
        le = jnp.where(ci <= jrow, 1.0, 0.0)
        tj = jnp.sum(le, axis=0, keepdims=True)
        r = jrow - jnp.sum(le * (ci - cx), axis=0, keepdims=True)
        onehot = jnp.where(trow == tj, 1.0, 0.0).astype(BF16)
        g = jnp.dot(lc.T.astype(BF16), onehot, preferred_element_type=F32)
        lo = jnp.sum(jnp.where(g <= r, 1.0, 0.0), axis=0, keepdims=True)
        idx_ref[pl.ds(e, 1), :] = (tj * LANES + lo).astype(I32)
        aft = af.T
        a1 = aft.astype(BF16)
        r1 = aft - a1.astype(F32)
        a2 = r1.astype(BF16)
        a3 = (r1 - a2.astype(F32)).astype(BF16)
        ag = (jnp.dot(a1, onehot, preferred_element_type=F32) + jnp.dot(a2, onehot, preferred_element_type=F32)
              + jnp.dot(a3, onehot, preferred_element_type=F32))
        gate_ref[pl.ds(e, 1), :] = jnp.sum(jnp.where(trow == lo, ag, 0.0), axis=0, keepdims=True)
        return carry

    lax.fori_loop(0, E, per_expert, 0)


def select_tokens(aff_t):
    B, E, S = aff_t.shape
    cap = CAPACITY_FACTOR * S // E
    nt = S // LANES
    consts = _select_consts(S)
    const_specs = [pl.BlockSpec(c.shape, lambda b: (0, 0)) for c in consts]
    return pl.pallas_call(
        functools.partial(_select_kernel, S=S, cap=cap),
        grid=(B,),
        in_specs=[pl.BlockSpec((None, E, S), lambda b: (b, 0, 0))] + const_specs,
        out_specs=[
            pl.BlockSpec((None, E, cap), lambda b: (b, 0, 0)),
            pl.BlockSpec((None, E, cap), lambda b: (b, 0, 0)),
            pl.BlockSpec((None, E, LANES), lambda b: (b, 0, 0)),
        ],
        out_shape=[
            jax.ShapeDtypeStruct((B, E, cap), I32),
            jax.ShapeDtypeStruct((B, E, cap), F32),
            jax.ShapeDtypeStruct((B, E, LANES), I32),
        ],
        scratch_shapes=[
            pltpu.VMEM((LANES * E, LANES), F32),
            pltpu.VMEM((LANES * E, LANES), F32),
            pltpu.VMEM((nt * E, LANES), F32),
            pltpu.VMEM((nt * E, LANES), F32),
        ],
        compiler_params=_cparams(("arbitrary",)),
        name="select",
    )(aff_t, *consts)


GATHER_UNROLL = 8
FFN_ROWS = 256


FFN_FSPLIT = 2


def _ffn_kernel(idx_ref, gate_ref, hp_ref, wg_ref, wu_ref, wd_ref, y_ref, xg_ref, yacc_ref, *, cap):
    f = pl.program_id(2)

    @pl.when(f == 0)
    def _():
        def gather(t, c):
            j0 = t * GATHER_UNROLL
            rows = [hp_ref[pl.ds(idx_ref[0, j0 + r], 1), :] for r in range(GATHER_UNROLL)]
            for r in range(GATHER_UNROLL):
                xg_ref[pl.ds(j0 + r, 1), :] = rows[r]
            return c

        lax.fori_loop(0, cap // GATHER_UNROLL, gather, 0)
        yacc_ref[...] = jnp.zeros_like(yacc_ref)

    half = wg_ref.shape[0] // 2
    wg = wg_ref[...].astype(BF16)
    wu = wu_ref[...].astype(BF16)
    wd = wd_ref[...].astype(BF16)
    rows = min(FFN_ROWS, cap)
    for r in range(cap // rows):
        sl = slice(r * rows, (r + 1) * rows)
        xp = xg_ref[sl, :]
        lo = pltpu.bitcast(xp << 16, F32).astype(BF16)
        hi = pltpu.bitcast(xp & jnp.uint32(0xFFFF0000), F32).astype(BF16)
        a = (jnp.dot(lo, wg[:half, :], preferred_element_type=F32)
             + jnp.dot(hi, wg[half:, :], preferred_element_type=F32))
        u = (jnp.dot(lo, wu[:half, :], preferred_element_type=F32)
             + jnp.dot(hi, wu[half:, :], preferred_element_type=F32))
        hid = ((a * jax.nn.sigmoid(a)) * u).astype(BF16)
        yacc_ref[sl, :] += jnp.dot(hid, wd, preferred_element_type=F32)

    @pl.when(f == FFN_FSPLIT - 1)
    def _():
        gcol = jnp.broadcast_to(gate_ref[...], (LANES, cap)).T[:, 0:1]
        ntile = wd_ref.shape[1] // LANES
        for t in range(ntile):
            ls = slice(t * LANES, (t + 1) * LANES)
            y_ref[pl.ds(t, cap, stride=ntile), :] = yacc_ref[:, ls] * gcol


def expert_ffn(idx, gate, hp, wg, wu, wd, layer):
    B, E, cap = idx.shape
    _, S, Dh = hp.shape
    D = 2 * Dh
    Fdim = wg.shape[-1]
    assert FFN_FSPLIT == 2
    fb = Fdim // FFN_FSPLIT
    return pl.pallas_call(
        functools.partial(_ffn_kernel, cap=cap),
        grid=(B, E, FFN_FSPLIT),
        in_specs=[
            pl.BlockSpec((None, 1, cap), lambda b, e, f: (b * E + e, 0, 0), memory_space=pltpu.SMEM),
            pl.BlockSpec((None, 1, cap), lambda b, e, f: (b * E + e, 0, 0)),
            pl.BlockSpec((None, S, Dh), lambda b, e, f: (b, 0, 0), pipeline_mode=pl.Buffered(1)),
            pl.BlockSpec((None, None, D, fb), lambda b, e, f: (layer, e, 0, f)),
            pl.BlockSpec((None, None, D, fb), lambda b, e, f: (layer, e, 0, f)),
            pl.BlockSpec((None, None, fb, D), lambda b, e, f: (layer, e, f, 0)),
        ],
        out_specs=pl.BlockSpec((None, cap * D // LANES, LANES), lambda b, e, f: (b * E + e, 0, 0)),
        out_shape=jax.ShapeDtypeStruct((B * E, cap * D // LANES, LANES), F32),
        scratch_shapes=[pltpu.VMEM((cap, Dh), jnp.uint32), pltpu.VMEM((cap, D), F32)],
        compiler_params=_cparams(("arbitrary", "arbitrary", "arbitrary")),
        name="ffn",
    )(idx.reshape(B * E, 1, cap), gate.reshape(B * E, 1, cap), hp, wg, wu, wd)


COMBINE_EG = 4
COMBINE_BR = 128
COMBINE_NBUF = 2
COMBINE_UNROLL = 4


def _combine_kernel(idx_ref, ts_ref, x_ref, gate_ref, y_hbm, o_ref, ybuf_ref, sem, be_ref, bk_ref, acc_ref,
                    *, tc, cap, n_e):
    b = pl.program_id(0)
    c = pl.program_id(1)
    g = pl.program_id(2)
    eg, br = COMBINE_EG, COMBINE_BR
    tiles = tc // LANES
    base = c * tc
    nl = x_ref.shape[1] // LANES

    @pl.when(g == 0)
    def _():
        acc_ref[...] = jnp.zeros_like(acc_ref)

    def slot_range(el):
        return ts_ref[el, c * tiles], ts_ref[el, (c + 1) * tiles]

    def build(el, n):
        lo, hi = slot_range(el)
        k0 = lo // br
        k1 = jnp.where(hi > lo, (hi + br - 1) // br, k0)

        def push(k, m):
            be_ref[m] = el
            bk_ref[m] = k
            return m + 1

        return lax.fori_loop(k0, k1, push, n)

    nblk = lax.fori_loop(0, eg, build, 0)

    def blk_copy(i, slot):
        row = b * n_e + g * eg + be_ref[i]
        return pltpu.make_async_copy(y_hbm.at[row, pl.ds(bk_ref[i] * (br * nl), br * nl), :], ybuf_ref.at[slot],
                                     sem.at[slot])

    @pl.when(nblk > 0)
    def _():
        blk_copy(0, 0).start()

    def per_block(i, carry):
        slot = i % COMBINE_NBUF

        @pl.when(i + 1 < nblk)
        def _():
            blk_copy(i + 1, (i + 1) % COMBINE_NBUF).start()

        blk_copy(i, slot).wait()
        el = be_ref[i]
        r0 = bk_ref[i] * br
        lo, hi = slot_range(el)
        j0 = jnp.maximum(lo, r0)
        j1 = jnp.minimum(hi, r0 + br)

        def add_rows(j, n):
            offs = [pl.multiple_of((idx_ref[el, j + r] - base) * nl, nl) for r in range(n)]
            vals = [acc_ref[pl.ds(offs[r], nl), :]
                    + ybuf_ref[slot, pl.ds(pl.multiple_of((j + r - r0) * nl, nl), nl), :] for r in range(n)]
            for r in range(n):
                acc_ref[pl.ds(offs[r], nl), :] = vals[r]

        ngrp = (j1 - j0) // COMBINE_UNROLL

        def grp(t, cc):
            add_rows(j0 + t * COMBINE_UNROLL, COMBINE_UNROLL)
            return cc

        lax.fori_loop(0, ngrp, grp, 0)

        def one(j, cc):
            add_rows(j, 1)
            return cc

        lax.fori_loop(j0 + ngrp * COMBINE_UNROLL, j1, one, 0)
        return carry

    lax.fori_loop(0, nblk, per_block, 0)

    @pl.when(g == n_e // eg - 1)
    def _():
        for t in range(nl):
            ls = slice(t * LANES, (t + 1) * LANES)
            o_ref[:, ls] = x_ref[:, ls] + gate_ref[:, ls] * acc_ref[pl.ds(t, tc, stride=nl), :]


def combine(idx, tstart, x, gate, y, tc=2048):
    B, S, D = x.shape
    _, E, cap = idx.shape
    tc = min(tc, S)
    eg = COMBINE_EG
    ng = E // eg
    max_blocks = eg * (cap // COMBINE_BR + 1)
    return pl.pallas_call(
        functools.partial(_combine_kernel, tc=tc, cap=cap, n_e=E),
        grid=(B, S // tc, ng),
        in_specs=[
            pl.BlockSpec((None, eg, cap), lambda b, c, g: (b * ng + g, 0, 0), memory_space=pltpu.SMEM),
            pl.BlockSpec((None, eg, LANES), lambda b, c, g: (b * ng + g, 0, 0), memory_space=pltpu.SMEM),
            pl.BlockSpec((None, tc, D), lambda b, c, g: (b, c, 0)),
            pl.BlockSpec((None, 1, D), lambda b, c, g: (b, 0, 0)),
            pl.BlockSpec(memory_space=pl.ANY),
        ],
        out_specs=pl.BlockSpec((None, tc, D), lambda b, c, g: (b, c, 0)),
        out_shape=jax.ShapeDtypeStruct((B, S, D), F32),
        scratch_shapes=[
            pltpu.VMEM((COMBINE_NBUF, COMBINE_BR * D // LANES, LANES), F32),
            pltpu.SemaphoreType.DMA((COMBINE_NBUF,)),
            pltpu.SMEM((max_blocks,), I32),
            pltpu.SMEM((max_blocks,), I32),
            pltpu.VMEM((tc * D // LANES, LANES), F32),
        ],
        compiler_params=_cparams(("arbitrary", "arbitrary", "arbitrary")),
        name="combine",
    )(idx.reshape(B * ng, eg, cap), tstart.reshape(B * ng, eg, LANES), x, gate, y)


def _final_kernel(x_ref, g_ref, o_ref):
    x = x_ref[...]
    o_ref[...] = x * lax.rsqrt(jnp.mean(x * x, axis=-1, keepdims=True) + EPS) * g_ref[...]


def final_norm(x, g, tm=1024):
    B, S, D = x.shape
    return pl.pallas_call(
        _final_kernel,
        grid=(B, S // tm),
        in_specs=[pl.BlockSpec((None, tm, D), lambda b, i: (b, i, 0)), pl.BlockSpec((1, D), lambda b, i: (0, 0))],
        out_specs=pl.BlockSpec((None, tm, D), lambda b, i: (b, i, 0)),
        out_shape=jax.ShapeDtypeStruct((B, S, D), F32),
        compiler_params=_cparams(("arbitrary", "arbitrary")),
        name="final_norm",
    )(x, g)


def _pad_lanes(a):
    return jnp.pad(a, ((0, 0), (0, LANES - a.shape[-1])))


def moe_block(x, h, aff_t, gate, wg, wu, wd, layer):
    idx, g, tstart = select_tokens(aff_t)
    y = expert_ffn(idx, g, h, wg, wu, wd, layer)
    return combine(idx, tstart, x, gate, y)


def kernel(x, c, positions, ada_w, ada_b, mix_norm_g, ffn_norm_g, fc_w_in, conv_w, conv_b, conv_ln_g, conv_ln_b, fc_w_out, attn_w_qkv, attn_sink, attn_w_out, router_w, router_b, moe_w_gate, moe_w_up, moe_w_down, final_norm_g):
    B, S, D = x.shape
    depth = ada_w.shape[0]
    mod = adaln_all(c, ada_w.reshape(depth * 2, D, 3 * D), ada_b.reshape(depth * 2, 1, 3 * D))
    mod = mod.reshape(depth, 2, B, 1, 3 * D)
    cos, sin = rope_tables(positions)
    for l in range(depth):
        i = l // 2
        shift, scale, gate = (mod[l, 0, :, :, k * D:(k + 1) * D] for k in range(3))
        shift2, scale2, gate2 = (mod[l, 1, :, :, k * D:(k + 1) * D] for k in range(3))
        g1 = mix_norm_g[l][None, :]
        if l % 2 == 0:
            uf, glu = even_in(x, g1, scale, shift, fc_w_in[i].astype(BF16))
            yf = fourier_mix(uf)
            yc = conv_ln_swish(glu, conv_w[i], conv_b[i][None, :], conv_ln_g[i][None, :], conv_ln_b[i][None, :])
            wo = fc_w_out[i].astype(BF16)
            ys, ws = [yf, yc], [wo[:FOURIER_WIDTH], wo[FOURIER_WIDTH:]]
        else:
            q, k, v = odd_in(x, g1, scale, shift, attn_w_qkv[i].astype(BF16), cos, sin)
            ys, ws = [attention(q, k, v, attn_sink[i])], [attn_w_out[i].astype(BF16)]
        rw = _pad_lanes(router_w[l])
        rwh = rw.astype(BF16)
        rwl = (rw - rwh.astype(F32)).astype(BF16)
        x, h, aff_t = post_mixer(ys, ws, x, gate, ffn_norm_g[l][None, :], scale2, shift2,
                                 rwh, rwl, _pad_lanes(router_b[l][None, :]))
        x = moe_block(x, h, aff_t, gate2, moe_w_gate, moe_w_up, moe_w_down, l)
    return final_norm(x, final_norm_g[None, :])
```

```python
import functools

import numpy as np
import jax
import jax.numpy as jnp
from jax import lax
from jax.experimental import pallas as pl
from jax.experimental.pallas import tpu as pltpu

F32 = jnp.float32
BF16 = jnp.bfloat16
I32 = jnp.int32

EPS = 1e-6
HEAD_DIM = 64
N_Q_HEADS = 16
N_KV_HEADS = 4
GQA = N_Q_HEADS // N_KV_HEADS
BLOCK = 128
N_EXPERTS = 16
CAPACITY_FACTOR = 2
CONV_KERNEL = 31
FOURIER_WIDTH = 512
CONV_WIDTH = 512
GROUP_CH = 128
ROPE_THETA = 10000.0
LANES = 128
SUBLANES = 8
FFT_N2 = 256
HALO = 16
VMEM_LIMIT = 56 * 1024 * 1024


def _cparams(sem):
    return pltpu.CompilerParams(dimension_semantics=sem, vmem_limit_bytes=VMEM_LIMIT)


def _rms_mod(x, g, scale, shift):
    y = x * lax.rsqrt(jnp.mean(x * x, axis=-1, keepdims=True) + EPS)
    return (y * g) * (1.0 + scale) + shift


def _adaln_kernel(ct_ref, w_ref, b_ref, o_ref):
    ct = ct_ref[...]
    cond = ct * jax.nn.sigmoid(ct)
    w = w_ref[...]
    rows = [jnp.sum(w * cond[:, b:b + 1], axis=0, keepdims=True) for b in range(ct.shape[1])]
    o_ref[...] = jnp.concatenate(rows, axis=0) + b_ref[...]


def adaln_all(c, ada_w, ada_b):
    B, D = c.shape
    L, _, N = ada_w.shape
    tn = 1024
    return pl.pallas_call(
        _adaln_kernel,
        grid=(L, N // tn),
        in_specs=[
            pl.BlockSpec((D, B), lambda l, j: (0, 0)),
            pl.BlockSpec((None, D, tn), lambda l, j: (l, 0, j)),
            pl.BlockSpec((None, 1, tn), lambda l, j: (l, 0, j)),
        ],
        out_specs=pl.BlockSpec((None, B, tn), lambda l, j: (l, 0, j)),
        out_shape=jax.ShapeDtypeStruct((L, B, N), F32),
        compiler_params=_cparams(("arbitrary", "arbitrary")),
        name="adaln",
    )(c.T, ada_w, ada_b)


def _even_in_kernel(x_ref, g_ref, sc_ref, sh_ref, w_ref, uf_ref, glu_ref):
    h = _rms_mod(x_ref[...], g_ref[...], sc_ref[...], sh_ref[...])
    p = jnp.dot(h.astype(BF16), w_ref[...], preferred_element_type=F32)
    fw, cw = FOURIER_WIDTH, CONV_WIDTH
    uf_ref[...] = p[:, :fw].astype(BF16)
    glu_ref[...] = p[:, fw:fw + cw] * jax.nn.sigmoid(p[:, fw + cw:])


def even_in(x, g, scale, shift, w, tm=512):
    B, S, D = x.shape
    N = w.shape[1]
    return pl.pallas_call(
        _even_in_kernel,
        grid=(B, S // tm),
        in_specs=[
            pl.BlockSpec((None, tm, D), lambda b, i: (b, i, 0)),
            pl.BlockSpec((1, D), lambda b, i: (0, 0)),
            pl.BlockSpec((None, 1, D), lambda b, i: (b, 0, 0)),
            pl.BlockSpec((None, 1, D), lambda b, i: (b, 0, 0)),
            pl.BlockSpec((D, N), lambda b, i: (0, 0)),
        ],
        out_specs=[
            pl.BlockSpec((None, tm, FOURIER_WIDTH), lambda b, i: (b, i, 0)),
            pl.BlockSpec((None, tm, CONV_WIDTH), lambda b, i: (b, i, 0)),
        ],
        out_shape=[
            jax.ShapeDtypeStruct((B, S, FOURIER_WIDTH), BF16),
            jax.ShapeDtypeStruct((B, S, CONV_WIDTH), F32),
        ],
        compiler_params=_cparams(("arbitrary", "arbitrary")),
        name="even_in",
    )(x, g, scale, shift, w)


def _fourier_consts(S):
    n2 = FFT_N2
    n1 = S // n2
    C = GROUP_CH
    scale = 1.0 / np.sqrt(float(S) * C)
    cm = 2.0 * np.pi * np.outer(np.arange(C), np.arange(C)) / C
    ccs = np.concatenate([np.cos(cm), -np.sin(cm)], axis=1) * scale
    ac = 2.0 * np.pi * np.outer(np.arange(n1), np.arange(n1)) / n1
    eye = np.eye(SUBLANES)
    kcat = np.concatenate([np.kron(np.cos(ac), eye), np.kron(-np.sin(ac), eye)], axis=0)
    cc = np.repeat(np.arange(n1), SUBLANES)[:, None]
    ii = np.tile(np.arange(SUBLANES), n1)[:, None]
    jj = np.arange(n2 // SUBLANES)[None, :]
    ang = 2.0 * np.pi * (SUBLANES * jj + ii) * cc / S
    twr, twi = np.cos(ang), -np.sin(ang)
    bd = 2.0 * np.pi * np.outer(np.arange(n2), np.arange(n2)) / n2
    w2 = np.concatenate([np.cos(bd), np.sin(bd)], axis=1)
    f = lambda a: jnp.asarray(a, F32)
    return f(ccs).astype(BF16), f(kcat).astype(BF16), f(twr), f(twi), f(w2).astype(BF16)


def _fourier_kernel(u_ref, ccs_ref, kcat_ref, twr_ref, twi_ref, w2_ref, o_ref, z_ref, y_ref, *, S):
    n2 = FFT_N2
    n1 = S // n2
    R = SUBLANES * n1
    C = GROUP_CH
    ch = min(S, 512)
    for r in range(S // ch):
        z_ref[r * ch:(r + 1) * ch, :] = jnp.dot(u_ref[r * ch:(r + 1) * ch, :], ccs_ref[...],
                                                preferred_element_type=F32)
    kcat = kcat_ref[...]
    for j in range(n2 // SUBLANES):
        r0 = SUBLANES * j
        d = jnp.concatenate([z_ref[a * n2 + r0:a * n2 + r0 + SUBLANES, :] for a in range(n1)], axis=0)
        p = jnp.dot(kcat, d.astype(BF16), preferred_element_type=F32)
        o_r = p[:R, :C] - p[R:, C:]
        o_i = p[R:, :C] + p[:R, C:]
        tr = twr_ref[:, j:j + 1]
        ti = twi_ref[:, j:j + 1]
        a_r = o_r * tr - o_i * ti
        a_i = o_r * ti + o_i * tr
        for c in range(n1):
            z_ref[c * n2 + r0:c * n2 + r0 + SUBLANES, :C] = a_r[SUBLANES * c:SUBLANES * (c + 1), :]
            z_ref[c * n2 + r0:c * n2 + r0 + SUBLANES, C:] = a_i[SUBLANES * c:SUBLANES * (c + 1), :]
    w2 = w2_ref[...]
    for c in range(n1):
        a = z_ref[c * n2:(c + 1) * n2, :]
        rhs = jnp.concatenate([a[:, :C], a[:, C:]], axis=0).astype(BF16)
        y_ref[pl.ds(c, n2, stride=n1), :] = jnp.dot(w2, rhs, preferred_element_type=F32)
    o_ref[...] = y_ref[...].astype(BF16)


def fourier_mix(uf):
    B, S, W = uf.shape
    consts = _fourier_consts(S)
    const_specs = [pl.BlockSpec(c.shape, lambda b, g: (0, 0)) for c in consts]
    return pl.pallas_call(
        functools.partial(_fourier_kernel, S=S),
        grid=(B, W // GROUP_CH),
        in_specs=[pl.BlockSpec((None, S, GROUP_CH), lambda b, g: (b, 0, g))] + const_specs,
        out_specs=pl.BlockSpec((None, S, GROUP_CH), lambda b, g: (b, 0, g)),
        out_shape=jax.ShapeDtypeStruct((B, S, W), BF16),
        scratch_shapes=[pltpu.VMEM((S, 2 * GROUP_CH), F32), pltpu.VMEM((S, GROUP_CH), F32)],
        compiler_params=_cparams(("arbitrary", "arbitrary")),
        name="fourier",
    )(uf, *consts)


def _conv_kernel(prev_ref, cur_ref, next_ref, w_ref, b_ref, g_ref, beta_ref, o_ref, ext_ref, sh_ref, *, tm, nt):
    i = pl.program_id(1)
    ext_ref[0:HALO, :] = jnp.where(i > 0, prev_ref[...], 0.0)
    ext_ref[HALO:HALO + tm, :] = cur_ref[...]
    ext_ref[HALO + tm:2 * HALO + tm, :] = jnp.where(i < nt - 1, next_ref[...], 0.0)
    pad = CONV_KERNEL // 2
    span = tm + 2 * HALO - SUBLANES
    for r in range(1, SUBLANES):
        sh_ref[r, 0:span, :] = ext_ref[r:r + span, :]
    acc = None
    for t in range(CONV_KERNEL):
        o = HALO - pad + t
        q, r = o // SUBLANES * SUBLANES, o % SUBLANES
        rows = ext_ref[q:q + tm, :] if r == 0 else sh_ref[r, q:q + tm, :]
        term = rows * w_ref[t:t + 1, :]
        acc = term if acc is None else acc + term
    conv = acc + b_ref[...]
    mu = jnp.mean(conv, axis=-1, keepdims=True)
    var = jnp.mean(jnp.square(conv - mu), axis=-1, keepdims=True)
    z = (conv - mu) * lax.rsqrt(var + EPS) * g_ref[...] + beta_ref[...]
    o_ref[...] = (z * jax.nn.sigmoid(z)).astype(BF16)


def conv_ln_swish(glu, w, b, g, beta, tm=512):
    B, S, C = glu.shape
    nt = S // tm
    hb = tm // HALO
    nh = S // HALO
    return pl.pallas_call(
        functools.partial(_conv_kernel, tm=tm, nt=nt),
        grid=(B, nt),
        in_specs=[
            pl.BlockSpec((None, HALO, C), lambda bb, i: (bb, jnp.maximum(i * hb - 1, 0), 0)),
            pl.BlockSpec((None, tm, C), lambda bb, i: (bb, i, 0)),
            pl.BlockSpec((None, HALO, C), lambda bb, i: (bb, jnp.minimum((i + 1) * hb, nh - 1), 0)),
            pl.BlockSpec((CONV_KERNEL, C), lambda bb, i: (0, 0)),
            pl.BlockSpec((1, C), lambda bb, i: (0, 0)),
            pl.BlockSpec((1, C), lambda bb, i: (0, 0)),
            pl.BlockSpec((1, C), lambda bb, i: (0, 0)),
        ],
        out_specs=pl.BlockSpec((None, tm, C), lambda bb, i: (bb, i, 0)),
        out_shape=jax.ShapeDtypeStruct((B, S, C), BF16),
        scratch_shapes=[pltpu.VMEM((tm + 2 * HALO, C), F32), pltpu.VMEM((SUBLANES, tm + 2 * HALO, C), F32)],
        compiler_params=_cparams(("arbitrary", "arbitrary")),
        name="conv",
    )(glu, glu, glu, w, b, g, beta)


def _rope_tab_kernel(pos_ref, inv_ref, sgn_ref, cos_ref, sin_ref):
    ang = pos_ref[...].astype(F32) * inv_ref[...]
    cos_ref[...] = jnp.cos(ang)
    sin_ref[...] = jnp.sin(ang) * sgn_ref[...]


def rope_tables(positions, ts=1024):
    B, S = positions.shape
    half = HEAD_DIM // 2
    inv = ROPE_THETA ** (-jnp.arange(0, HEAD_DIM, 2, dtype=F32) / HEAD_DIM)
    reps = LANES // half
    inv_row = jnp.tile(inv, reps)[None, :]
    sgn_row = jnp.tile(jnp.concatenate([-jnp.ones((half,), F32), jnp.ones((half,), F32)]), reps // 2)[None, :]
    return pl.pallas_call(
        _rope_tab_kernel,
        grid=(B, S // ts),
        in_specs=[
            pl.BlockSpec((None, ts, 1), lambda b, i: (b, i, 0)),
            pl.BlockSpec((1, LANES), lambda b, i: (0, 0)),
            pl.BlockSpec((1, LANES), lambda b, i: (0, 0)),
        ],
        out_specs=[pl.BlockSpec((None, ts, LANES), lambda b, i: (b, i, 0))] * 2,
        out_shape=[jax.ShapeDtypeStruct((B, S, LANES), F32)] * 2,
        compiler_params=_cparams(("arbitrary", "arbitrary")),
        name="rope_tab",
    )(positions.reshape(B, S, 1), inv_row, sgn_row)


def _rope_block(t, cos, sin_signed, first_half):
    half = HEAD_DIM // 2
    rot = jnp.where(first_half, pltpu.roll(t, LANES - half, 1), pltpu.roll(t, half, 1))
    return t * cos + rot * sin_signed


def _odd_in_kernel(x_ref, g_ref, sc_ref, sh_ref, w_ref, cos_ref, sin_ref, q_ref, k_ref, v_ref):
    h = _rms_mod(x_ref[...], g_ref[...], sc_ref[...], sh_ref[...])
    p = jnp.dot(h.astype(BF16), w_ref[...], preferred_element_type=F32)
    qd = N_Q_HEADS * HEAD_DIM
    kd = N_KV_HEADS * HEAD_DIM
    cos = cos_ref[...]
    sin = sin_ref[...]
    lane = lax.broadcasted_iota(I32, cos.shape, 1)
    first_half = (lane % HEAD_DIM) < (HEAD_DIM // 2)
    qscale = HEAD_DIM ** -0.5
    for j in range(qd // LANES):
        blk = _rope_block(p[:, j * LANES:(j + 1) * LANES], cos, sin, first_half)
        q_ref[:, j * LANES:(j + 1) * LANES] = (blk * qscale).astype(BF16)
    for j in range(kd // LANES):
        blk = _rope_block(p[:, qd + j * LANES:qd + (j + 1) * LANES], cos, sin, first_half)
        k_ref[:, j * LANES:(j + 1) * LANES] = blk.astype(BF16)
    v_ref[...] = p[:, qd + kd:].astype(BF16)


def odd_in(x, g, scale, shift, w, cos, sin, tm=512):
    B, S, D = x.shape
    N = w.shape[1]
    qd = N_Q_HEADS * HEAD_DIM
    kd = N_KV_HEADS * HEAD_DIM
    return pl.pallas_call(
        _odd_in_kernel,
        grid=(B, S // tm),
        in_specs=[
            pl.BlockSpec((None, tm, D), lambda b, i: (b, i, 0)),
            pl.BlockSpec((1, D), lambda b, i: (0, 0)),
            pl.BlockSpec((None, 1, D), lambda b, i: (b, 0, 0)),
            pl.BlockSpec((None, 1, D), lambda b, i: (b, 0, 0)),
            pl.BlockSpec((D, N), lambda b, i: (0, 0)),
            pl.BlockSpec((None, tm, LANES), lambda b, i: (b, i, 0)),
            pl.BlockSpec((None, tm, LANES), lambda b, i: (b, i, 0)),
        ],
        out_specs=[
            pl.BlockSpec((None, tm, qd), lambda b, i: (b, i, 0)),
            pl.BlockSpec((None, tm, kd), lambda b, i: (b, i, 0)),
            pl.BlockSpec((None, tm, kd), lambda b, i: (b, i, 0)),
        ],
        out_shape=[
            jax.ShapeDtypeStruct((B, S, qd), BF16),
            jax.ShapeDtypeStruct((B, S, kd), BF16),
            jax.ShapeDtypeStruct((B, S, kd), BF16),
        ],
        compiler_params=_cparams(("arbitrary", "arbitrary")),
        name="odd_in",
    )(x, g, scale, shift, w, cos, sin)


def _attn_kernel(sink_ref, q_ref, kp_ref, kc_ref, kn_ref, vp_ref, vc_ref, vn_ref, o_ref, *, nb):
    n = pl.program_id(1)
    T = BLOCK
    kwin = jnp.concatenate([kp_ref[...], kc_ref[...], kn_ref[...]], axis=0)
    vwin = jnp.concatenate([vp_ref[...], vc_ref[...], vn_ref[...]], axis=0)
    qi = lax.broadcasted_iota(I32, (T, 3 * T), 0)
    kj = lax.broadcasted_iota(I32, (T, 3 * T), 1)
    rel = kj - T - qi
    kpos = (n - 1) * T + kj
    valid1 = (jnp.abs(rel) <= BLOCK) & (kpos >= 0) & (kpos < nb * T)
    valid = jnp.concatenate([valid1] * GQA, axis=0)
    grp = lax.broadcasted_iota(I32, (T, GQA * HEAD_DIM), 1) // HEAD_DIM
    left = lax.broadcasted_iota(I32, (T, LANES), 1) < HEAD_DIM
    ones = jnp.ones((3 * T, LANES), BF16)
    for kh in range(N_KV_HEADS):
        kk = kwin[:, kh * HEAD_DIM:(kh + 1) * HEAD_DIM]
        vv = vwin[:, kh * HEAD_DIM:(kh + 1) * HEAD_DIM]
        k4 = jnp.concatenate([kk] * GQA, axis=1)
        v1 = jnp.concatenate([vv, vv, ones], axis=1)
        qh = q_ref[:, kh * GQA * HEAD_DIM:(kh + 1) * GQA * HEAD_DIM]
        qm = jnp.concatenate([jnp.where(grp == g, qh, jnp.zeros_like(qh)) for g in range(GQA)], axis=0)
        s = lax.dot_general(qm, k4, (((1,), (1,)), ((), ())), preferred_element_type=F32)
        s = jnp.where(valid, s, -jnp.inf)
        sink = jnp.concatenate(
            [jnp.full((T, 1), sink_ref[kh * GQA + g], F32) for g in range(GQA)], axis=0)
        m = jnp.maximum(jnp.max(s, axis=-1, keepdims=True), sink)
        e = jnp.exp(s - m).astype(BF16)
        od = jnp.dot(e, v1, preferred_element_type=F32)
        o = od[:, :LANES] / (od[:, LANES:] + jnp.exp(sink - m))
        for t in range(GQA // 2):
            tile = jnp.where(left, o[2 * t * T:(2 * t + 1) * T], o[(2 * t + 1) * T:(2 * t + 2) * T])
            c0 = kh * GQA * HEAD_DIM + t * LANES
            o_ref[:, c0:c0 + LANES] = tile.astype(BF16)


def attention(q, k, v, sink):
    B, S, qd = q.shape
    kd = k.shape[-1]
    nb = S // BLOCK
    prev = lambda b, n: (b, jnp.maximum(n - 1, 0), 0)
    cur = lambda b, n: (b, n, 0)
    nxt = lambda b, n: (b, jnp.minimum(n + 1, nb - 1), 0)
    kv = lambda im: pl.BlockSpec((None, BLOCK, kd), im)
    return pl.pallas_call(
        functools.partial(_attn_kernel, nb=nb),
        grid=(B, nb),
        in_specs=[
            pl.BlockSpec(memory_space=pltpu.SMEM),
            pl.BlockSpec((None, BLOCK, qd), cur),
            kv(prev), kv(cur), kv(nxt), kv(prev), kv(cur), kv(nxt),
        ],
        out_specs=pl.BlockSpec((None, BLOCK, qd), cur),
        out_shape=jax.ShapeDtypeStruct((B, S, qd), BF16),
        compiler_params=_cparams(("arbitrary", "arbitrary")),
        name="attn",
    )(sink, q, k, k, k, v, v, v)


def _post_kernel(*refs, n_y):
    y_refs = refs[:n_y]
    w_refs = refs[n_y:2 * n_y]
    x_ref, gate_ref, g2_ref, sc2_ref, sh2_ref, rwh_ref, rwl_ref, rb_ref = refs[2 * n_y:2 * n_y + 8]
    xo_ref, h_ref, aff_ref = refs[2 * n_y + 8:]
    acc = jnp.dot(y_refs[0][...], w_refs[0][...], preferred_element_type=F32)
    for yr, wr in zip(y_refs[1:], w_refs[1:]):
        acc = acc + jnp.dot(yr[...], wr[...], preferred_element_type=F32)
    xn = x_ref[...] + gate_ref[...] * acc
    xo_ref[...] = xn
    h = _rms_mod(xn, g2_ref[...], sc2_ref[...], sh2_ref[...])
    hh = h.astype(BF16)
    half = h.shape[1] // 2
    lo = pltpu.bitcast(hh[:, :half].astype(F32), jnp.uint32) >> 16
    hi = pltpu.bitcast(hh[:, half:].astype(F32), jnp.uint32) & jnp.uint32(0xFFFF0000)
    packed = lo | hi
    nl = half // LANES
    for t in range(nl):
        h_ref[pl.ds(t, h.shape[0], stride=nl), :] = packed[:, t * LANES:(t + 1) * LANES]
    hl = (h - hh.astype(F32)).astype(BF16)
    rwh = rwh_ref[...]
    logits = (jnp.dot(hh, rwh, preferred_element_type=F32)
              + jnp.dot(hl, rwh, preferred_element_type=F32)
              + jnp.dot(hh, rwl_ref[...], preferred_element_type=F32)) + rb_ref[...]
    lane = lax.broadcasted_iota(I32, logits.shape, 1)
    logits = jnp.where(lane < N_EXPERTS, logits, -jnp.inf)
    m = jnp.max(logits, axis=-1, keepdims=True)
    e = jnp.exp(logits - m)
    aff = e / jnp.sum(e, axis=-1, keepdims=True)
    aff_ref[...] = aff.T[:N_EXPERTS, :]


def post_mixer(ys, ws, x, gate, g2, sc2, sh2, rwh, rwl, rb, tm=512):
    B, S, D = x.shape
    n_y = len(ys)
    in_specs = [pl.BlockSpec((None, tm, y.shape[-1]), lambda b, i: (b, i, 0)) for y in ys]
    in_specs += [pl.BlockSpec(w.shape, lambda b, i: (0, 0)) for w in ws]
    in_specs += [
        pl.BlockSpec((None, tm, D), lambda b, i: (b, i, 0)),
        pl.BlockSpec((None, 1, D), lambda b, i: (b, 0, 0)),
        pl.BlockSpec((1, D), lambda b, i: (0, 0)),
        pl.BlockSpec((None, 1, D), lambda b, i: (b, 0, 0)),
        pl.BlockSpec((None, 1, D), lambda b, i: (b, 0, 0)),
        pl.BlockSpec((D, LANES), lambda b, i: (0, 0)),
        pl.BlockSpec((D, LANES), lambda b, i: (0, 0)),
        pl.BlockSpec((1, LANES), lambda b, i: (0, 0)),
    ]
    return pl.pallas_call(
        functools.partial(_post_kernel, n_y=n_y),
        grid=(B, S // tm),
        in_specs=in_specs,
        out_specs=[
            pl.BlockSpec((None, tm, D), lambda b, i: (b, i, 0)),
            pl.BlockSpec((None, tm * (D // 2) // LANES, LANES), lambda b, i: (b, i, 0)),
            pl.BlockSpec((None, N_EXPERTS, tm), lambda b, i: (b, 0, i)),
        ],
        out_shape=[
            jax.ShapeDtypeStruct((B, S, D), F32),
            jax.ShapeDtypeStruct((B, S * (D // 2) // LANES, LANES), jnp.uint32),
            jax.ShapeDtypeStruct((B, N_EXPERTS, S), F32),
        ],
        compiler_params=_cparams(("arbitrary", "arbitrary")),
        name="post",
    )(*ys, *ws, x, gate, g2, sc2, sh2, rwh, rwl, rb)


def _select_consts(S):
    nt = S // LANES
    E = N_EXPERTS
    li = np.arange(LANES)
    u_incl = (li[:, None] <= li[None, :]).astype(np.float32)
    u_strict = (li[:, None] < li[None, :]).astype(np.float32)
    r = np.arange(nt * E)
    tt, ee = r // E, r % E
    l_strict = ((ee[:, None] == ee[None, :]) & (tt[None, :] < tt[:, None])).astype(np.float32)
    tind = (np.arange(S)[:, None] // LANES == li[None, :]).astype(np.float32)
    f = lambda a: jnp.asarray(a, F32).astype(BF16)
    return f(u_incl), f(u_strict), f(l_strict), f(tind)


def _select_kernel(aff_ref, ui_ref, us_ref, ls_ref, tind_ref, idx_ref, gate_ref, ts_ref,
                   affx_ref, lcs_ref, cex_ref, cin_ref, *, S, cap):
    E = N_EXPERTS
    nt = S // LANES
    aff = aff_ref[...]
    v = pltpu.bitcast(aff, I32)
    capf = jnp.float32(cap)

    def count(mask):
        return jnp.sum(jnp.where(mask, 1.0, 0.0), axis=1, keepdims=True)

    def search(i, t):
        cand = t | jnp.left_shift(jnp.int32(1), 30 - i)
        return jnp.where(count(v >= cand) >= capf, cand, t)

    thr = lax.fori_loop(0, 31, search, jnp.zeros((E, 1), I32))
    gt = v > thr
    eq = v == thr
    need = capf - count(gt)

    def to_tiles(m):
        return jnp.concatenate([m[:, LANES * t:LANES * (t + 1)] for t in range(nt)], axis=0)

    def prefix(mt):
        lcs = jnp.dot(mt.astype(BF16), ui_ref[...], preferred_element_type=F32)
        off = jnp.dot(ls_ref[...], lcs.astype(BF16), preferred_element_type=F32)[:, LANES - 1:LANES]
        return lcs, off

    eqt = to_tiles(jnp.where(eq, 1.0, 0.0))
    lcs_e, off_e = prefix(eqt)
    rank_t = lcs_e + off_e - eqt
    rank = jnp.concatenate([rank_t[E * t:E * (t + 1), :] for t in range(nt)], axis=1)
    sel = gt | (eq & (rank < need))
    self_ = jnp.where(sel, 1.0, 0.0)

    tot_et = jnp.dot(self_.astype(BF16), tind_ref[...], preferred_element_type=F32)
    ts_ref[...] = jnp.dot(tot_et.astype(BF16), us_ref[...], preferred_element_type=F32).astype(I32)

    selt = to_tiles(self_)
    lcs, off = prefix(selt)
    nrow = nt * E
    zeros = jnp.zeros((nrow, LANES), F32)
    pad_rows = LANES * E - nrow
    lcs_ref[0:nrow, :] = lcs
    lcs_ref[nrow:LANES * E, :] = jnp.zeros((pad_rows, LANES), F32)
    affx_ref[0:nrow, :] = to_tiles(aff)
    affx_ref[nrow:LANES * E, :] = jnp.zeros((pad_rows, LANES), F32)
    cex_ref[...] = off + zeros
    cin_ref[...] = off + lcs[:, LANES - 1:LANES] + zeros

    jrow = lax.broadcasted_iota(I32, (1, cap), 1).astype(F32)
    trow = lax.broadcasted_iota(I32, (LANES, 1), 0).astype(F32)

    def per_expert(e, carry):
        lc = lcs_ref[pl.ds(e, LANES, stride=E), :]
        af = affx_ref[pl.ds(e, LANES, stride=E), :]
        cx = cex_ref[pl.ds(e, nt, stride=E), :][:, 0:1]
        ci = cin_ref[pl.ds(e, nt, stride=E), :][:, 0:1]
        le = jnp.where(ci <= jrow, 1.0, 0.0)
        tj = jnp.sum(le, axis=0, keepdims=True)
        r = jrow - jnp.sum(le * (ci - cx), axis=0, keepdims=True)
        onehot = jnp.where(trow == tj, 1.0, 0.0).astype(BF16)
        g = jnp.dot(lc.T.astype(BF16), onehot, preferred_element_type=F32)
        lo = jnp.sum(jnp.where(g <= r, 1.0, 0.0), axis=0, keepdims=True)
        idx_ref[pl.ds(e, 1), :] = (tj * LANES + lo).astype(I32)
        aft = af.T
        a1 = aft.astype(BF16)
        r1 = aft - a1.astype(F32)
        a2 = r1.astype(BF16)
        a3 = (r1 - a2.astype(F32)).astype(BF16)
        ag = (jnp.dot(a1, onehot, preferred_element_type=F32) + jnp.dot(a2, onehot, preferred_element_type=F32)
              + jnp.dot(a3, onehot, preferred_element_type=F32))
        gate_ref[pl.ds(e, 1), :] = jnp.sum(jnp.where(trow == lo, ag, 0.0), axis=0, keepdims=True)
        return carry

    lax.fori_loop(0, E, per_expert, 0)


def select_tokens(aff_t):
    B, E, S = aff_t.shape
    cap = CAPACITY_FACTOR * S // E
    nt = S // LANES
    consts = _select_consts(S)
    const_specs = [pl.BlockSpec(c.shape, lambda b: (0, 0)) for c in consts]
    return pl.pallas_call(
        functools.partial(_select_kernel, S=S, cap=cap),
        grid=(B,),
        in_specs=[pl.BlockSpec((None, E, S), lambda b: (b, 0, 0))] + const_specs,
        out_specs=[
            pl.BlockSpec((None, E, cap), lambda b: (b, 0, 0)),
            pl.BlockSpec((None, E, cap), lambda b: (b, 0, 0)),
            pl.BlockSpec((None, E, LANES), lambda b: (b, 0, 0)),
        ],
        out_shape=[
            jax.ShapeDtypeStruct((B, E, cap), I32),
            jax.ShapeDtypeStruct((B, E, cap), F32),
            jax.ShapeDtypeStruct((B, E, LANES), I32),
        ],
        scratch_shapes=[
            pltpu.VMEM((LANES * E, LANES), F32),
            pltpu.VMEM((LANES * E, LANES), F32),
            pltpu.VMEM((nt * E, LANES), F32),
            pltpu.VMEM((nt * E, LANES), F32),
        ],
        compiler_params=_cparams(("arbitrary",)),
        name="select",
    )(aff_t, *consts)


GATHER_UNROLL = 8
FFN_ROWS = 256


FFN_FSPLIT = 2


def _ffn_kernel(idx_ref, gate_ref, hp_ref, wg_ref, wu_ref, wd_ref, y_ref, xg_ref, yacc_ref, *, cap):
    f = pl.program_id(2)
    half = wg_ref.shape[0] // 2
    nl = half // LANES

    @pl.when(f == 0)
    def _():
        def gather(t, c):
            j0 = t * GATHER_UNROLL
            rows = [hp_ref[pl.ds(pl.multiple_of(idx_ref[0, j0 + r] * nl, nl), nl), :] for r in range(GATHER_UNROLL)]
            for r in range(GATHER_UNROLL):
                xg_ref[pl.ds(pl.multiple_of((j0 + r) * nl, nl), nl), :] = rows[r]
            return c

        lax.fori_loop(0, cap // GATHER_UNROLL, gather, 0)
        yacc_ref[...] = jnp.zeros_like(yacc_ref)

    wg = wg_ref[...].astype(BF16)
    wu = wu_ref[...].astype(BF16)
    wd = wd_ref[...].astype(BF16)
    rows = min(FFN_ROWS, cap)
    for r in range(cap // rows):
        sl = slice(r * rows, (r + 1) * rows)
        xp = jnp.concatenate([xg_ref[pl.ds(r * rows * nl + t, rows, stride=nl), :] for t in range(nl)],
                             axis=1)
        lo = pltpu.bitcast(xp << 16, F32).astype(BF16)
        hi = pltpu.bitcast(xp & jnp.uint32(0xFFFF0000), F32).astype(BF16)
        a = (jnp.dot(lo, wg[:half, :], preferred_element_type=F32)
             + jnp.dot(hi, wg[half:, :], preferred_element_type=F32))
        u = (jnp.dot(lo, wu[:half, :], preferred_element_type=F32)
             + jnp.dot(hi, wu[half:, :], preferred_element_type=F32))
        hid = ((a * jax.nn.sigmoid(a)) * u).astype(BF16)
        yacc_ref[sl, :] += jnp.dot(hid, wd, preferred_element_type=F32)

    @pl.when(f == FFN_FSPLIT - 1)
    def _():
        gcol = jnp.broadcast_to(gate_ref[...], (LANES, cap)).T[:, 0:1]
        ntile = wd_ref.shape[1] // LANES
        for t in range(ntile):
            ls = slice(t * LANES, (t + 1) * LANES)
            y_ref[pl.ds(t, cap, stride=ntile), :] = yacc_ref[:, ls] * gcol


def expert_ffn(idx, gate, hp, wg, wu, wd, layer):
    B, E, cap = idx.shape
    D, Fdim = wg.shape[-2:]
    Dh = D // 2
    hrows = hp.shape[1]
    assert FFN_FSPLIT == 2
    fb = Fdim // FFN_FSPLIT
    return pl.pallas_call(
        functools.partial(_ffn_kernel, cap=cap),
        grid=(B, E, FFN_FSPLIT),
        in_specs=[
            pl.BlockSpec((None, 1, cap), lambda b, e, f: (b * E + e, 0, 0), memory_space=pltpu.SMEM),
            pl.BlockSpec((None, 1, cap), lambda b, e, f: (b * E + e, 0, 0)),
            pl.BlockSpec((None, hrows, LANES), lambda b, e, f: (b, 0, 0), pipeline_mode=pl.Buffered(1)),
            pl.BlockSpec((None, None, D, fb), lambda b, e, f: (layer, e, 0, f)),
            pl.BlockSpec((None, None, D, fb), lambda b, e, f: (layer, e, 0, f)),
            pl.BlockSpec((None, None, fb, D), lambda b, e, f: (layer, e, f, 0)),
        ],
        out_specs=pl.BlockSpec((None, cap * D // LANES, LANES), lambda b, e, f: (b * E + e, 0, 0)),
        out_shape=jax.ShapeDtypeStruct((B * E, cap * D // LANES, LANES), F32),
        scratch_shapes=[pltpu.VMEM((cap * Dh // LANES, LANES), jnp.uint32), pltpu.VMEM((cap, D), F32)],
        compiler_params=_cparams(("arbitrary", "arbitrary", "arbitrary")),
        name="ffn",
    )(idx.reshape(B * E, 1, cap), gate.reshape(B * E, 1, cap), hp, wg, wu, wd)


COMBINE_EG = 4
COMBINE_BR = 128
COMBINE_NBUF = 6
COMBINE_UNROLL = 4


def _combine_kernel(idx_ref, ts_ref, x_ref, gate_ref, y_hbm, o_ref, ybuf_ref, sem, be_ref, bk_ref, acc_ref,
                    *, tc, cap, n_e):
    b = pl.program_id(0)
    c = pl.program_id(1)
    g = pl.program_id(2)
    eg, br = COMBINE_EG, COMBINE_BR
    tiles = tc // LANES
    base = c * tc
    nl = x_ref.shape[1] // LANES

    @pl.when(g == 0)
    def _():
        acc_ref[...] = jnp.zeros_like(acc_ref)

    def slot_range(el):
        return ts_ref[el, c * tiles], ts_ref[el, (c + 1) * tiles]

    def build(el, n):
        lo, hi = slot_range(el)
        k0 = lo // br
        k1 = jnp.where(hi > lo, (hi + br - 1) // br, k0)

        def push(k, m):
            be_ref[m] = el
            bk_ref[m] = k
            return m + 1

        return lax.fori_loop(k0, k1, push, n)

    nblk = lax.fori_loop(0, eg, build, 0)

    def blk_copy(i, slot):
        row = b * n_e + g * eg + be_ref[i]
        return pltpu.make_async_copy(y_hbm.at[row, pl.ds(bk_ref[i] * (br * nl), br * nl), :], ybuf_ref.at[slot],
                                     sem.at[slot])

    ahead = COMBINE_NBUF - 1
    for i0 in range(ahead):
        @pl.when(i0 < nblk)
        def _():
            blk_copy(i0, i0).start()

    def per_block(i, carry):
        slot = i % COMBINE_NBUF

        @pl.when(i + ahead < nblk)
        def _():
            blk_copy(i + ahead, (i + ahead) % COMBINE_NBUF).start()

        blk_copy(i, slot).wait()
        el = be_ref[i]
        r0 = bk_ref[i] * br
        lo, hi = slot_range(el)
        j0 = jnp.maximum(lo, r0)
        j1 = jnp.minimum(hi, r0 + br)

        def add_rows(j, n):
            offs = [pl.multiple_of((idx_ref[el, j + r] - base) * nl, nl) for r in range(n)]
            vals = [acc_ref[pl.ds(offs[r], nl), :]
                    + ybuf_ref[slot, pl.ds(pl.multiple_of((j + r - r0) * nl, nl), nl), :] for r in range(n)]
            for r in range(n):
                acc_ref[pl.ds(offs[r], nl), :] = vals[r]

        ngrp = (j1 - j0) // COMBINE_UNROLL

        def grp(t, cc):
            add_rows(j0 + t * COMBINE_UNROLL, COMBINE_UNROLL)
            return cc

        lax.fori_loop(0, ngrp, grp, 0)

        def one(j, cc):
            add_rows(j, 1)
            return cc

        lax.fori_loop(j0 + ngrp * COMBINE_UNROLL, j1, one, 0)
        return carry

    lax.fori_loop(0, nblk, per_block, 0)

    @pl.when(g == n_e // eg - 1)
    def _():
        for t in range(nl):
            ls = slice(t * LANES, (t + 1) * LANES)
            o_ref[:, ls] = x_ref[:, ls] + gate_ref[:, ls] * acc_ref[pl.ds(t, tc, stride=nl), :]


def combine(idx, tstart, x, gate, y, tc=2048):
    B, S, D = x.shape
    _, E, cap = idx.shape
    tc = min(tc, S)
    eg = COMBINE_EG
    ng = E // eg
    max_blocks = eg * (cap // COMBINE_BR + 1)
    return pl.pallas_call(
        functools.partial(_combine_kernel, tc=tc, cap=cap, n_e=E),
        grid=(B, S // tc, ng),
        in_specs=[
            pl.BlockSpec((None, eg, cap), lambda b, c, g: (b * ng + g, 0, 0), memory_space=pltpu.SMEM),
            pl.BlockSpec((None, eg, LANES), lambda b, c, g: (b * ng + g, 0, 0), memory_space=pltpu.SMEM),
            pl.BlockSpec((None, tc, D), lambda b, c, g: (b, c, 0)),
            pl.BlockSpec((None, 1, D), lambda b, c, g: (b, 0, 0)),
            pl.BlockSpec(memory_space=pl.ANY),
        ],
        out_specs=pl.BlockSpec((None, tc, D), lambda b, c, g: (b, c, 0)),
        out_shape=jax.ShapeDtypeStruct((B, S, D), F32),
        scratch_shapes=[
            pltpu.VMEM((COMBINE_NBUF, COMBINE_BR * D // LANES, LANES), F32),
            pltpu.SemaphoreType.DMA((COMBINE_NBUF,)),
            pltpu.SMEM((max_blocks,), I32),
            pltpu.SMEM((max_blocks,), I32),
            pltpu.VMEM((tc * D // LANES, LANES), F32),
        ],
        compiler_params=_cparams(("arbitrary", "arbitrary", "arbitrary")),
        name="combine",
    )(idx.reshape(B * ng, eg, cap), tstart.reshape(B * ng, eg, LANES), x, gate, y)


def _final_kernel(x_ref, g_ref, o_ref):
    x = x_ref[...]
    o_ref[...] = x * lax.rsqrt(jnp.mean(x * x, axis=-1, keepdims=True) + EPS) * g_ref[...]


def final_norm(x, g, tm=1024):
    B, S, D = x.shape
    return pl.pallas_call(
        _final_kernel,
        grid=(B, S // tm),
        in_specs=[pl.BlockSpec((None, tm, D), lambda b, i: (b, i, 0)), pl.BlockSpec((1, D), lambda b, i: (0, 0))],
        out_specs=pl.BlockSpec((None, tm, D), lambda b, i: (b, i, 0)),
        out_shape=jax.ShapeDtypeStruct((B, S, D), F32),
        compiler_params=_cparams(("arbitrary", "arbitrary")),
        name="final_norm",
    )(x, g)


def _pad_lanes(a):
    return jnp.pad(a, ((0, 0), (0, LANES - a.shape[-1])))


def moe_block(x, h, aff_t, gate, wg, wu, wd, layer):
    idx, g, tstart = select_tokens(aff_t)
    y = expert_ffn(idx, g, h, wg, wu, wd, layer)
    return combine(idx, tstart, x, gate, y)


def kernel(x, c, positions, ada_w, ada_b, mix_norm_g, ffn_norm_g, fc_w_in, conv_w, conv_b, conv_ln_g, conv_ln_b, fc_w_out, attn_w_qkv, attn_sink, attn_w_out, router_w, router_b, moe_w_gate, moe_w_up, moe_w_down, final_norm_g):
    B, S, D = x.shape
    depth = ada_w.shape[0]
    mod = adaln_all(c, ada_w.reshape(depth * 2, D, 3 * D), ada_b.reshape(depth * 2, 1, 3 * D))
    mod = mod.reshape(depth, 2, B, 1, 3 * D)
    cos, sin = rope_tables(positions)
    for l in range(depth):
        i = l // 2
        shift, scale, gate = (mod[l, 0, :, :, k * D:(k + 1) * D] for k in range(3))
        shift2, scale2, gate2 = (mod[l, 1, :, :, k * D:(k + 1) * D] for k in range(3))
        g1 = mix_norm_g[l][None, :]
        if l % 2 == 0:
            uf, glu = even_in(x, g1, scale, shift, fc_w_in[i].astype(BF16))
            yf = fourier_mix(uf)
            yc = conv_ln_swish(glu, conv_w[i], conv_b[i][None, :], conv_ln_g[i][None, :], conv_ln_b[i][None, :])
            wo = fc_w_out[i].astype(BF16)
            ys, ws = [yf, yc], [wo[:FOURIER_WIDTH], wo[FOURIER_WIDTH:]]
        else:
            q, k, v = odd_in(x, g1, scale, shift, attn_w_qkv[i].astype(BF16), cos, sin)
            ys, ws = [attention(q, k, v, attn_sink[i])], [attn_w_out[i].astype(BF16)]
        rw = _pad_lanes(router_w[l])
        rwh = rw.astype(BF16)
        rwl = (rw - rwh.astype(F32)).astype(BF16)
        x, h, aff_t = post_mixer(ys, ws, x, gate, ffn_norm_g[l][None, :], scale2, shift2,
                                 rwh, rwl, _pad_lanes(router_b[l][None, :]))
        x = moe_block(x, h, aff_t, gate2, moe_w_gate, moe_w_up, moe_w_down, l)
    return final_norm(x, final_norm_g[None, :])
```

```python
import functools

import numpy as np
import jax
import jax.numpy as jnp
from jax import lax
from jax.experimental import pallas as pl
from jax.experimental.pallas import tpu as pltpu

F32 = jnp.float32
BF16 = jnp.bfloat16
I32 = jnp.int32

EPS = 1e-6
HEAD_DIM = 64
N_Q_HEADS = 16
N_KV_HEADS = 4
GQA = N_Q_HEADS // N_KV_HEADS
BLOCK = 128
N_EXPERTS = 16
CAPACITY_FACTOR = 2
CONV_KERNEL = 31
FOURIER_WIDTH = 512
CONV_WIDTH = 512
GROUP_CH = 128
ROPE_THETA = 10000.0
LANES = 128
SUBLANES = 8
FFT_N2 = 256
HALO = 16
VMEM_LIMIT = 56 * 1024 * 1024


def _cparams(sem):
    return pltpu.CompilerParams(dimension_semantics=sem, vmem_limit_bytes=VMEM_LIMIT)


def _rms_mod(x, g, scale, shift):
    y = x * lax.rsqrt(jnp.mean(x * x, axis=-1, keepdims=True) + EPS)
    return (y * g) * (1.0 + scale) + shift


def _adaln_kernel(ct_ref, w_ref, b_ref, o_ref):
    ct = ct_ref[...]
    cond = ct * jax.nn.sigmoid(ct)
    w = w_ref[...]
    rows = [jnp.sum(w * cond[:, b:b + 1], axis=0, keepdims=True) for b in range(ct.shape[1])]
    o_ref[...] = jnp.concatenate(rows, axis=0) + b_ref[...]


def adaln_all(c, ada_w, ada_b):
    B, D = c.shape
    L, _, N = ada_w.shape
    tn = 1024
    return pl.pallas_call(
        _adaln_kernel,
        grid=(L, N // tn),
        in_specs=[
            pl.BlockSpec((D, B), lambda l, j: (0, 0)),
            pl.BlockSpec((None, D, tn), lambda l, j: (l, 0, j)),
            pl.BlockSpec((None, 1, tn), lambda l, j: (l, 0, j)),
        ],
        out_specs=pl.BlockSpec((None, B, tn), lambda l, j: (l, 0, j)),
        out_shape=jax.ShapeDtypeStruct((L, B, N), F32),
        compiler_params=_cparams(("arbitrary", "arbitrary")),
        name="adaln",
    )(c.T, ada_w, ada_b)


def _even_in_kernel(x_ref, g_ref, sc_ref, sh_ref, w_ref, uf_ref, glu_ref):
    h = _rms_mod(x_ref[...], g_ref[...], sc_ref[...], sh_ref[...])
    p = jnp.dot(h.astype(BF16), w_ref[...], preferred_element_type=F32)
    fw, cw = FOURIER_WIDTH, CONV_WIDTH
    uf_ref[...] = p[:, :fw].astype(BF16)
    glu_ref[...] = p[:, fw:fw + cw] * jax.nn.sigmoid(p[:, fw + cw:])


def even_in(x, g, scale, shift, w, tm=512):
    B, S, D = x.shape
    N = w.shape[1]
    return pl.pallas_call(
        _even_in_kernel,
        grid=(B, S // tm),
        in_specs=[
            pl.BlockSpec((None, tm, D), lambda b, i: (b, i, 0)),
            pl.BlockSpec((1, D), lambda b, i: (0, 0)),
            pl.BlockSpec((None, 1, D), lambda b, i: (b, 0, 0)),
            pl.BlockSpec((None, 1, D), lambda b, i: (b, 0, 0)),
            pl.BlockSpec((D, N), lambda b, i: (0, 0)),
        ],
        out_specs=[
            pl.BlockSpec((None, tm, FOURIER_WIDTH), lambda b, i: (b, i, 0)),
            pl.BlockSpec((None, tm, CONV_WIDTH), lambda b, i: (b, i, 0)),
        ],
        out_shape=[
            jax.ShapeDtypeStruct((B, S, FOURIER_WIDTH), BF16),
            jax.ShapeDtypeStruct((B, S, CONV_WIDTH), F32),
        ],
        compiler_params=_cparams(("arbitrary", "arbitrary")),
        name="even_in",
    )(x, g, scale, shift, w)


def _fourier_consts(S):
    n2 = FFT_N2
    n1 = S // n2
    C = GROUP_CH
    scale = 1.0 / np.sqrt(float(S) * C)
    cm = 2.0 * np.pi * np.outer(np.arange(C), np.arange(C)) / C
    ccs = np.concatenate([np.cos(cm), -np.sin(cm)], axis=1) * scale
    ac = 2.0 * np.pi * np.outer(np.arange(n1), np.arange(n1)) / n1
    eye = np.eye(SUBLANES)
    kcat = np.concatenate([np.kron(np.cos(ac), eye), np.kron(-np.sin(ac), eye)], axis=0)
    cc = np.repeat(np.arange(n1), SUBLANES)[:, None]
    ii = np.tile(np.arange(SUBLANES), n1)[:, None]
    jj = np.arange(n2 // SUBLANES)[None, :]
    ang = 2.0 * np.pi * (SUBLANES * jj + ii) * cc / S
    twr, twi = np.cos(ang), -np.sin(ang)
    bd = 2.0 * np.pi * np.outer(np.arange(n2), np.arange(n2)) / n2
    w2 = np.concatenate([np.cos(bd), np.sin(bd)], axis=1)
    f = lambda a: jnp.asarray(a, F32)
    return f(ccs).astype(BF16), f(kcat).astype(BF16), f(twr), f(twi), f(w2).astype(BF16)


def _fourier_kernel(u_ref, ccs_ref, kcat_ref, twr_ref, twi_ref, w2_ref, o_ref, z_ref, y_ref, *, S):
    n2 = FFT_N2
    n1 = S // n2
    R = SUBLANES * n1
    C = GROUP_CH
    ch = min(S, 512)
    for r in range(S // ch):
        z_ref[r * ch:(r + 1) * ch, :] = jnp.dot(u_ref[r * ch:(r + 1) * ch, :], ccs_ref[...],
                                                preferred_element_type=F32)
    kcat = kcat_ref[...]
    for j in range(n2 // SUBLANES):
        r0 = SUBLANES * j
        d = jnp.concatenate([z_ref[a * n2 + r0:a * n2 + r0 + SUBLANES, :] for a in range(n1)], axis=0)
        p = jnp.dot(kcat, d.astype(BF16), preferred_element_type=F32)
        o_r = p[:R, :C] - p[R:, C:]
        o_i = p[R:, :C] + p[:R, C:]
        tr = twr_ref[:, j:j + 1]
        ti = twi_ref[:, j:j + 1]
        a_r = o_r * tr - o_i * ti
        a_i = o_r * ti + o_i * tr
        for c in range(n1):
            z_ref[c * n2 + r0:c * n2 + r0 + SUBLANES, :C] = a_r[SUBLANES * c:SUBLANES * (c + 1), :]
            z_ref[c * n2 + r0:c * n2 + r0 + SUBLANES, C:] = a_i[SUBLANES * c:SUBLANES * (c + 1), :]
    w2 = w2_ref[...]
    for c in range(n1):
        a = z_ref[c * n2:(c + 1) * n2, :]
        rhs = jnp.concatenate([a[:, :C], a[:, C:]], axis=0).astype(BF16)
        y_ref[pl.ds(c, n2, stride=n1), :] = jnp.dot(w2, rhs, preferred_element_type=F32)
    o_ref[...] = y_ref[...].astype(BF16)


def fourier_mix(uf):
    B, S, W = uf.shape
    consts = _fourier_consts(S)
    const_specs = [pl.BlockSpec(c.shape, lambda b, g: (0, 0)) for c in consts]
    return pl.pallas_call(
        functools.partial(_fourier_kernel, S=S),
        grid=(B, W // GROUP_CH),
        in_specs=[pl.BlockSpec((None, S, GROUP_CH), lambda b, g: (b, 0, g))] + const_specs,
        out_specs=pl.BlockSpec((None, S, GROUP_CH), lambda b, g: (b, 0, g)),
        out_shape=jax.ShapeDtypeStruct((B, S, W), BF16),
        scratch_shapes=[pltpu.VMEM((S, 2 * GROUP_CH), F32), pltpu.VMEM((S, GROUP_CH), F32)],
        compiler_params=_cparams(("arbitrary", "arbitrary")),
        name="fourier",
    )(uf, *consts)


def _conv_kernel(prev_ref, cur_ref, next_ref, w_ref, b_ref, g_ref, beta_ref, o_ref, ext_ref, sh_ref, *, tm, nt):
    i = pl.program_id(1)
    ext_ref[0:HALO, :] = jnp.where(i > 0, prev_ref[...], 0.0)
    ext_ref[HALO:HALO + tm, :] = cur_ref[...]
    ext_ref[HALO + tm:2 * HALO + tm, :] = jnp.where(i < nt - 1, next_ref[...], 0.0)
    pad = CONV_KERNEL // 2
    span = tm + 2 * HALO - SUBLANES
    for r in range(1, SUBLANES):
        sh_ref[r, 0:span, :] = ext_ref[r:r + span, :]
    acc = None
    for t in range(CONV_KERNEL):
        o = HALO - pad + t
        q, r = o // SUBLANES * SUBLANES, o % SUBLANES
        rows = ext_ref[q:q + tm, :] if r == 0 else sh_ref[r, q:q + tm, :]
        term = rows * w_ref[t:t + 1, :]
        acc = term if acc is None else acc + term
    conv = acc + b_ref[...]
    mu = jnp.mean(conv, axis=-1, keepdims=True)
    var = jnp.mean(jnp.square(conv - mu), axis=-1, keepdims=True)
    z = (conv - mu) * lax.rsqrt(var + EPS) * g_ref[...] + beta_ref[...]
    o_ref[...] = (z * jax.nn.sigmoid(z)).astype(BF16)


def conv_ln_swish(glu, w, b, g, beta, tm=512):
    B, S, C = glu.shape
    nt = S // tm
    hb = tm // HALO
    nh = S // HALO
    return pl.pallas_call(
        functools.partial(_conv_kernel, tm=tm, nt=nt),
        grid=(B, nt),
        in_specs=[
            pl.BlockSpec((None, HALO, C), lambda bb, i: (bb, jnp.maximum(i * hb - 1, 0), 0)),
            pl.BlockSpec((None, tm, C), lambda bb, i: (bb, i, 0)),
            pl.BlockSpec((None, HALO, C), lambda bb, i: (bb, jnp.minimum((i + 1) * hb, nh - 1), 0)),
            pl.BlockSpec((CONV_KERNEL, C), lambda bb, i: (0, 0)),
            pl.BlockSpec((1, C), lambda bb, i: (0, 0)),
            pl.BlockSpec((1, C), lambda bb, i: (0, 0)),
            pl.BlockSpec((1, C), lambda bb, i: (0, 0)),
        ],
        out_specs=pl.BlockSpec((None, tm, C), lambda bb, i: (bb, i, 0)),
        out_shape=jax.ShapeDtypeStruct((B, S, C), BF16),
        scratch_shapes=[pltpu.VMEM((tm + 2 * HALO, C), F32), pltpu.VMEM((SUBLANES, tm + 2 * HALO, C), F32)],
        compiler_params=_cparams(("arbitrary", "arbitrary")),
        name="conv",
    )(glu, glu, glu, w, b, g, beta)


def _rope_tab_kernel(pos_ref, inv_ref, sgn_ref, cos_ref, sin_ref):
    ang = pos_ref[...].astype(F32) * inv_ref[...]
    cos_ref[...] = jnp.cos(ang)
    sin_ref[...] = jnp.sin(ang) * sgn_ref[...]


def rope_tables(positions, ts=1024):
    B, S = positions.shape
    half = HEAD_DIM // 2
    inv = ROPE_THETA ** (-jnp.arange(0, HEAD_DIM, 2, dtype=F32) / HEAD_DIM)
    reps = LANES // half
    inv_row = jnp.tile(inv, reps)[None, :]
    sgn_row = jnp.tile(jnp.concatenate([-jnp.ones((half,), F32), jnp.ones((half,), F32)]), reps // 2)[None, :]
    return pl.pallas_call(
        _rope_tab_kernel,
        grid=(B, S // ts),
        in_specs=[
            pl.BlockSpec((None, ts, 1), lambda b, i: (b, i, 0)),
            pl.BlockSpec((1, LANES), lambda b, i: (0, 0)),
            pl.BlockSpec((1, LANES), lambda b, i: (0, 0)),
        ],
        out_specs=[pl.BlockSpec((None, ts, LANES), lambda b, i: (b, i, 0))] * 2,
        out_shape=[jax.ShapeDtypeStruct((B, S, LANES), F32)] * 2,
        compiler_params=_cparams(("arbitrary", "arbitrary")),
        name="rope_tab",
    )(positions.reshape(B, S, 1), inv_row, sgn_row)


def _rope_block(t, cos, sin_signed, first_half):
    half = HEAD_DIM // 2
    rot = jnp.where(first_half, pltpu.roll(t, LANES - half, 1), pltpu.roll(t, half, 1))
    return t * cos + rot * sin_signed


def _odd_in_kernel(x_ref, g_ref, sc_ref, sh_ref, w_ref, cos_ref, sin_ref, q_ref, k_ref, v_ref):
    h = _rms_mod(x_ref[...], g_ref[...], sc_ref[...], sh_ref[...])
    p = jnp.dot(h.astype(BF16), w_ref[...], preferred_element_type=F32)
    qd = N_Q_HEADS * HEAD_DIM
    kd = N_KV_HEADS * HEAD_DIM
    cos = cos_ref[...]
    sin = sin_ref[...]
    lane = lax.broadcasted_iota(I32, cos.shape, 1)
    first_half = (lane % HEAD_DIM) < (HEAD_DIM // 2)
    qscale = HEAD_DIM ** -0.5
    for j in range(qd // LANES):
        blk = _rope_block(p[:, j * LANES:(j + 1) * LANES], cos, sin, first_half)
        q_ref[:, j * LANES:(j + 1) * LANES] = (blk * qscale).astype(BF16)
    for j in range(kd // LANES):
        blk = _rope_block(p[:, qd + j * LANES:qd + (j + 1) * LANES], cos, sin, first_half)
        k_ref[:, j * LANES:(j + 1) * LANES] = blk.astype(BF16)
    v_ref[...] = p[:, qd + kd:].astype(BF16)


def odd_in(x, g, scale, shift, w, cos, sin, tm=512):
    B, S, D = x.shape
    N = w.shape[1]
    qd = N_Q_HEADS * HEAD_DIM
    kd = N_KV_HEADS * HEAD_DIM
    return pl.pallas_call(
        _odd_in_kernel,
        grid=(B, S // tm),
        in_specs=[
            pl.BlockSpec((None, tm, D), lambda b, i: (b, i, 0)),
            pl.BlockSpec((1, D), lambda b, i: (0, 0)),
            pl.BlockSpec((None, 1, D), lambda b, i: (b, 0, 0)),
            pl.BlockSpec((None, 1, D), lambda b, i: (b, 0, 0)),
            pl.BlockSpec((D, N), lambda b, i: (0, 0)),
            pl.BlockSpec((None, tm, LANES), lambda b, i: (b, i, 0)),
            pl.BlockSpec((None, tm, LANES), lambda b, i: (b, i, 0)),
        ],
        out_specs=[
            pl.BlockSpec((None, tm, qd), lambda b, i: (b, i, 0)),
            pl.BlockSpec((None, tm, kd), lambda b, i: (b, i, 0)),
            pl.BlockSpec((None, tm, kd), lambda b, i: (b, i, 0)),
        ],
        out_shape=[
            jax.ShapeDtypeStruct((B, S, qd), BF16),
            jax.ShapeDtypeStruct((B, S, kd), BF16),
            jax.ShapeDtypeStruct((B, S, kd), BF16),
        ],
        compiler_params=_cparams(("arbitrary", "arbitrary")),
        name="odd_in",
    )(x, g, scale, shift, w, cos, sin)


def _attn_kernel(sink_ref, q_ref, kp_ref, kc_ref, kn_ref, vp_ref, vc_ref, vn_ref, o_ref, *, nb):
    n = pl.program_id(1)
    T = BLOCK
    kwin = jnp.concatenate([kp_ref[...], kc_ref[...], kn_ref[...]], axis=0)
    vwin = jnp.concatenate([vp_ref[...], vc_ref[...], vn_ref[...]], axis=0)
    qi = lax.broadcasted_iota(I32, (T, 3 * T), 0)
    kj = lax.broadcasted_iota(I32, (T, 3 * T), 1)
    rel = kj - T - qi
    kpos = (n - 1) * T + kj
    valid1 = (jnp.abs(rel) <= BLOCK) & (kpos >= 0) & (kpos < nb * T)
    valid = jnp.concatenate([valid1] * GQA, axis=0)
    grp = lax.broadcasted_iota(I32, (T, GQA * HEAD_DIM), 1) // HEAD_DIM
    left = lax.broadcasted_iota(I32, (T, LANES), 1) < HEAD_DIM
    ones = jnp.ones((3 * T, LANES), BF16)
    for kh in range(N_KV_HEADS):
        kk = kwin[:, kh * HEAD_DIM:(kh + 1) * HEAD_DIM]
        vv = vwin[:, kh * HEAD_DIM:(kh + 1) * HEAD_DIM]
        k4 = jnp.concatenate([kk] * GQA, axis=1)
        v1 = jnp.concatenate([vv, vv, ones], axis=1)
        qh = q_ref[:, kh * GQA * HEAD_DIM:(kh + 1) * GQA * HEAD_DIM]
        qm = jnp.concatenate([jnp.where(grp == g, qh, jnp.zeros_like(qh)) for g in range(GQA)], axis=0)
        s = lax.dot_general(qm, k4, (((1,), (1,)), ((), ())), preferred_element_type=F32)
        s = jnp.where(valid, s, -jnp.inf)
        sink = jnp.concatenate(
            [jnp.full((T, 1), sink_ref[kh * GQA + g], F32) for g in range(GQA)], axis=0)
        m = jnp.maximum(jnp.max(s, axis=-1, keepdims=True), sink)
        e = jnp.exp(s - m).astype(BF16)
        od = jnp.dot(e, v1, preferred_element_type=F32)
        o = od[:, :LANES] / (od[:, LANES:] + jnp.exp(sink - m))
        for t in range(GQA // 2):
            tile = jnp.where(left, o[2 * t * T:(2 * t + 1) * T], o[(2 * t + 1) * T:(2 * t + 2) * T])
            c0 = kh * GQA * HEAD_DIM + t * LANES
            o_ref[:, c0:c0 + LANES] = tile.astype(BF16)


def attention(q, k, v, sink):
    B, S, qd = q.shape
    kd = k.shape[-1]
    nb = S // BLOCK
    prev = lambda b, n: (b, jnp.maximum(n - 1, 0), 0)
    cur = lambda b, n: (b, n, 0)
    nxt = lambda b, n: (b, jnp.minimum(n + 1, nb - 1), 0)
    kv = lambda im: pl.BlockSpec((None, BLOCK, kd), im)
    return pl.pallas_call(
        functools.partial(_attn_kernel, nb=nb),
        grid=(B, nb),
        in_specs=[
            pl.BlockSpec(memory_space=pltpu.SMEM),
            pl.BlockSpec((None, BLOCK, qd), cur),
            kv(prev), kv(cur), kv(nxt), kv(prev), kv(cur), kv(nxt),
        ],
        out_specs=pl.BlockSpec((None, BLOCK, qd), cur),
        out_shape=jax.ShapeDtypeStruct((B, S, qd), BF16),
        compiler_params=_cparams(("arbitrary", "arbitrary")),
        name="attn",
    )(sink, q, k, k, k, v, v, v)


def _post_kernel(*refs, n_y):
    y_refs = refs[:n_y]
    w_refs = refs[n_y:2 * n_y]
    x_ref, gate_ref, g2_ref, sc2_ref, sh2_ref, rwh_ref, rwl_ref, rb_ref = refs[2 * n_y:2 * n_y + 8]
    xo_ref, h_ref, aff_ref = refs[2 * n_y + 8:]
    acc = jnp.dot(y_refs[0][...], w_refs[0][...], preferred_element_type=F32)
    for yr, wr in zip(y_refs[1:], w_refs[1:]):
        acc = acc + jnp.dot(yr[...], wr[...], preferred_element_type=F32)
    xn = x_ref[...] + gate_ref[...] * acc
    xo_ref[...] = xn
    h = _rms_mod(xn, g2_ref[...], sc2_ref[...], sh2_ref[...])
    hh = h.astype(BF16)
    half = h.shape[1] // 2
    lo = pltpu.bitcast(hh[:, :half].astype(F32), jnp.uint32) >> 16
    hi = pltpu.bitcast(hh[:, half:].astype(F32), jnp.uint32) & jnp.uint32(0xFFFF0000)
    packed = lo | hi
    nl = half // LANES
    for t in range(nl):
        h_ref[pl.ds(t, h.shape[0], stride=nl), :] = packed[:, t * LANES:(t + 1) * LANES]
    hl = (h - hh.astype(F32)).astype(BF16)
    rwh = rwh_ref[...]
    logits = (jnp.dot(hh, rwh, preferred_element_type=F32)
              + jnp.dot(hl, rwh, preferred_element_type=F32)
              + jnp.dot(hh, rwl_ref[...], preferred_element_type=F32)) + rb_ref[...]
    lane = lax.broadcasted_iota(I32, logits.shape, 1)
    logits = jnp.where(lane < N_EXPERTS, logits, -jnp.inf)
    m = jnp.max(logits, axis=-1, keepdims=True)
    e = jnp.exp(logits - m)
    aff = e / jnp.sum(e, axis=-1, keepdims=True)
    aff_ref[...] = aff.T[:N_EXPERTS, :]


def post_mixer(ys, ws, x, gate, g2, sc2, sh2, rwh, rwl, rb, tm=512):
    B, S, D = x.shape
    n_y = len(ys)
    in_specs = [pl.BlockSpec((None, tm, y.shape[-1]), lambda b, i: (b, i, 0)) for y in ys]
    in_specs += [pl.BlockSpec(w.shape, lambda b, i: (0, 0)) for w in ws]
    in_specs += [
        pl.BlockSpec((None, tm, D), lambda b, i: (b, i, 0)),
        pl.BlockSpec((None, 1, D), lambda b, i: (b, 0, 0)),
        pl.BlockSpec((1, D), lambda b, i: (0, 0)),
        pl.BlockSpec((None, 1, D), lambda b, i: (b, 0, 0)),
        pl.BlockSpec((None, 1, D), lambda b, i: (b, 0, 0)),
        pl.BlockSpec((D, LANES), lambda b, i: (0, 0)),
        pl.BlockSpec((D, LANES), lambda b, i: (0, 0)),
        pl.BlockSpec((1, LANES), lambda b, i: (0, 0)),
    ]
    return pl.pallas_call(
        functools.partial(_post_kernel, n_y=n_y),
        grid=(B, S // tm),
        in_specs=in_specs,
        out_specs=[
            pl.BlockSpec((None, tm, D), lambda b, i: (b, i, 0)),
            pl.BlockSpec((None, tm * (D // 2) // LANES, LANES), lambda b, i: (b, i, 0)),
            pl.BlockSpec((None, N_EXPERTS, tm), lambda b, i: (b, 0, i)),
        ],
        out_shape=[
            jax.ShapeDtypeStruct((B, S, D), F32),
            jax.ShapeDtypeStruct((B, S * (D // 2) // LANES, LANES), jnp.uint32),
            jax.ShapeDtypeStruct((B, N_EXPERTS, S), F32),
        ],
        compiler_params=_cparams(("arbitrary", "arbitrary")),
        name="post",
    )(*ys, *ws, x, gate, g2, sc2, sh2, rwh, rwl, rb)


def _select_consts(S):
    nt = S // LANES
    E = N_EXPERTS
    li = np.arange(LANES)
    u_incl = (li[:, None] <= li[None, :]).astype(np.float32)
    u_strict = (li[:, None] < li[None, :]).astype(np.float32)
    r = np.arange(nt * E)
    tt, ee = r // E, r % E
    l_strict = ((ee[:, None] == ee[None, :]) & (tt[None, :] < tt[:, None])).astype(np.float32)
    tind = (np.arange(S)[:, None] // LANES == li[None, :]).astype(np.float32)
    f = lambda a: jnp.asarray(a, F32).astype(BF16)
    return f(u_incl), f(u_strict), f(l_strict), f(tind)


def _select_kernel(aff_ref, ui_ref, us_ref, ls_ref, tind_ref, idx_ref, gate_ref, ts_ref,
                   affx_ref, lcs_ref, cex_ref, cin_ref, *, S, cap):
    E = N_EXPERTS
    nt = S // LANES
    aff = aff_ref[...]
    v = pltpu.bitcast(aff, I32)
    capf = jnp.float32(cap)

    def count(mask):
        return jnp.sum(jnp.where(mask, 1.0, 0.0), axis=1, keepdims=True)

    def search(i, t):
        cand = t | jnp.left_shift(jnp.int32(1), 30 - i)
        return jnp.where(count(v >= cand) >= capf, cand, t)

    thr = lax.fori_loop(0, 31, search, jnp.zeros((E, 1), I32))
    gt = v > thr
    eq = v == thr
    need = capf - count(gt)

    def to_tiles(m):
        return jnp.concatenate([m[:, LANES * t:LANES * (t + 1)] for t in range(nt)], axis=0)

    def prefix(mt):
        lcs = jnp.dot(mt.astype(BF16), ui_ref[...], preferred_element_type=F32)
        off = jnp.dot(ls_ref[...], lcs.astype(BF16), preferred_element_type=F32)[:, LANES - 1:LANES]
        return lcs, off

    eqt = to_tiles(jnp.where(eq, 1.0, 0.0))
    lcs_e, off_e = prefix(eqt)
    rank_t = lcs_e + off_e - eqt
    rank = jnp.concatenate([rank_t[E * t:E * (t + 1), :] for t in range(nt)], axis=1)
    sel = gt | (eq & (rank < need))
    self_ = jnp.where(sel, 1.0, 0.0)

    tot_et = jnp.dot(self_.astype(BF16), tind_ref[...], preferred_element_type=F32)
    ts_ref[...] = jnp.dot(tot_et.astype(BF16), us_ref[...], preferred_element_type=F32).astype(I32)

    selt = to_tiles(self_)
    lcs, off = prefix(selt)
    nrow = nt * E
    zeros = jnp.zeros((nrow, LANES), F32)
    pad_rows = LANES * E - nrow
    lcs_ref[0:nrow, :] = lcs
    lcs_ref[nrow:LANES * E, :] = jnp.zeros((pad_rows, LANES), F32)
    affx_ref[0:nrow, :] = to_tiles(aff)
    affx_ref[nrow:LANES * E, :] = jnp.zeros((pad_rows, LANES), F32)
    cex_ref[...] = off + zeros
    cin_ref[...] = off + lcs[:, LANES - 1:LANES] + zeros

    jrow = lax.broadcasted_iota(I32, (1, cap), 1).astype(F32)
    trow = lax.broadcasted_iota(I32, (LANES, 1), 0).astype(F32)

    def per_expert(e, carry):
        lc = lcs_ref[pl.ds(e, LANES, stride=E), :]
        af = affx_ref[pl.ds(e, LANES, stride=E), :]
        cx = cex_ref[pl.ds(e, nt, stride=E), :][:, 0:1]
        ci = cin_ref[pl.ds(e, nt, stride=E), :][:, 0:1]
        le = jnp.where(ci <= jrow, 1.0, 0.0)
        tj = jnp.sum(le, axis=0, keepdims=True)
        r = jrow - jnp.sum(le * (ci - cx), axis=0, keepdims=True)
        onehot = jnp.where(trow == tj, 1.0, 0.0).astype(BF16)
        g = jnp.dot(lc.T.astype(BF16), onehot, preferred_element_type=F32)
        lo = jnp.sum(jnp.where(g <= r, 1.0, 0.0), axis=0, keepdims=True)
        idx_ref[pl.ds(e, 1), :] = (tj * LANES + lo).astype(I32)
        aft = af.T
        a1 = aft.astype(BF16)
        r1 = aft - a1.astype(F32)
        a2 = r1.astype(BF16)
        a3 = (r1 - a2.astype(F32)).astype(BF16)
        ag = (jnp.dot(a1, onehot, preferred_element_type=F32) + jnp.dot(a2, onehot, preferred_element_type=F32)
              + jnp.dot(a3, onehot, preferred_element_type=F32))
        gate_ref[pl.ds(e, 1), :] = jnp.sum(jnp.where(trow == lo, ag, 0.0), axis=0, keepdims=True)
        return carry

    lax.fori_loop(0, E, per_expert, 0)


def select_tokens(aff_t):
    B, E, S = aff_t.shape
    cap = CAPACITY_FACTOR * S // E
    nt = S // LANES
    consts = _select_consts(S)
    const_specs = [pl.BlockSpec(c.shape, lambda b: (0, 0)) for c in consts]
    return pl.pallas_call(
        functools.partial(_select_kernel, S=S, cap=cap),
        grid=(B,),
        in_specs=[pl.BlockSpec((None, E, S), lambda b: (b, 0, 0))] + const_specs,
        out_specs=[
            pl.BlockSpec((None, E, cap), lambda b: (b, 0, 0)),
            pl.BlockSpec((None, E, cap), lambda b: (b, 0, 0)),
            pl.BlockSpec((None, E, LANES), lambda b: (b, 0, 0)),
        ],
        out_shape=[
            jax.ShapeDtypeStruct((B, E, cap), I32),
            jax.ShapeDtypeStruct((B, E, cap), F32),
            jax.ShapeDtypeStruct((B, E, LANES), I32),
        ],
        scratch_shapes=[
            pltpu.VMEM((LANES * E, LANES), F32),
            pltpu.VMEM((LANES * E, LANES), F32),
            pltpu.VMEM((nt * E, LANES), F32),
            pltpu.VMEM((nt * E, LANES), F32),
        ],
        compiler_params=_cparams(("arbitrary",)),
        name="select",
    )(aff_t, *consts)


GATHER_UNROLL = 8
FFN_ROWS = 256


def _gather_kernel(idx_ref, hp_ref, xg_ref, *, cap):
    nl = xg_ref.shape[0] // cap

    def gather(t, c):
        j0 = t * GATHER_UNROLL
        rows = [hp_ref[pl.ds(pl.multiple_of(idx_ref[0, j0 + r] * nl, nl), nl), :] for r in range(GATHER_UNROLL)]
        for r in range(GATHER_UNROLL):
            xg_ref[pl.ds(pl.multiple_of((j0 + r) * nl, nl), nl), :] = rows[r]
        return c

    lax.fori_loop(0, cap // GATHER_UNROLL, gather, 0)


def gather_rows(idx, hp, nl):
    B, E, cap = idx.shape
    hrows = hp.shape[1]
    return pl.pallas_call(
        functools.partial(_gather_kernel, cap=cap),
        grid=(B, E),
        in_specs=[
            pl.BlockSpec((None, 1, cap), lambda b, e: (b * E + e, 0, 0), memory_space=pltpu.SMEM),
            pl.BlockSpec((None, hrows, LANES), lambda b, e: (b, 0, 0), pipeline_mode=pl.Buffered(1)),
        ],
        out_specs=pl.BlockSpec((None, cap * nl, LANES), lambda b, e: (b * E + e, 0, 0)),
        out_shape=jax.ShapeDtypeStruct((B * E, cap * nl, LANES), jnp.uint32),
        compiler_params=_cparams(("arbitrary", "arbitrary")),
        name="gather",
    )(idx.reshape(B * E, 1, cap), hp)


def _ffn_kernel(gate_ref, xg_ref, wg_ref, wu_ref, wd_ref, y_ref, wgb_ref, wub_ref, wdb_ref, *, cap):
    @pl.when(pl.program_id(1) == 0)
    def _():
        wgb_ref[...] = wg_ref[...].astype(BF16)
        wub_ref[...] = wu_ref[...].astype(BF16)
        wdb_ref[...] = wd_ref[...].astype(BF16)

    half = wg_ref.shape[0] // 2
    nl = half // LANES
    ntile = wd_ref.shape[1] // LANES
    gcol = jnp.broadcast_to(gate_ref[...], (LANES, cap)).T[:, 0:1]
    rows = min(FFN_ROWS, cap)
    for r in range(cap // rows):
        sl = slice(r * rows, (r + 1) * rows)
        xp = jnp.concatenate([xg_ref[pl.ds(r * rows * nl + t, rows, stride=nl), :] for t in range(nl)],
                             axis=1)
        lo = pltpu.bitcast(xp << 16, F32).astype(BF16)
        hi = pltpu.bitcast(xp & jnp.uint32(0xFFFF0000), F32).astype(BF16)
        a = (jnp.dot(lo, wgb_ref[:half, :], preferred_element_type=F32)
             + jnp.dot(hi, wgb_ref[half:, :], preferred_element_type=F32))
        u = (jnp.dot(lo, wub_ref[:half, :], preferred_element_type=F32)
             + jnp.dot(hi, wub_ref[half:, :], preferred_element_type=F32))
        hid = ((a * jax.nn.sigmoid(a)) * u).astype(BF16)
        y = jnp.dot(hid, wdb_ref[...], preferred_element_type=F32) * gcol[sl, :]
        for t in range(ntile):
            y_ref[pl.ds(r * rows * ntile + t, rows, stride=ntile), :] = y[:, t * LANES:(t + 1) * LANES]


def expert_ffn(gate, xg, wg, wu, wd, layer):
    B, E, cap = gate.shape
    D, Fdim = wg.shape[-2:]
    xrows = xg.shape[1]
    return pl.pallas_call(
        functools.partial(_ffn_kernel, cap=cap),
        grid=(E, B),
        in_specs=[
            pl.BlockSpec((None, 1, cap), lambda e, b: (b * E + e, 0, 0)),
            pl.BlockSpec((None, xrows, LANES), lambda e, b: (b * E + e, 0, 0)),
            pl.BlockSpec((None, None, D, Fdim), lambda e, b: (layer, e, 0, 0)),
            pl.BlockSpec((None, None, D, Fdim), lambda e, b: (layer, e, 0, 0)),
            pl.BlockSpec((None, None, Fdim, D), lambda e, b: (layer, e, 0, 0)),
        ],
        out_specs=pl.BlockSpec((None, cap * D // LANES, LANES), lambda e, b: (b * E + e, 0, 0)),
        out_shape=jax.ShapeDtypeStruct((B * E, cap * D // LANES, LANES), F32),
        scratch_shapes=[pltpu.VMEM((D, Fdim), BF16), pltpu.VMEM((D, Fdim), BF16), pltpu.VMEM((Fdim, D), BF16)],
        compiler_params=_cparams(("arbitrary", "arbitrary")),
        name="ffn",
    )(gate.reshape(B * E, 1, cap), xg, wg, wu, wd)


COMBINE_EG = 4
COMBINE_BR = 128
COMBINE_NBUF = 6
COMBINE_UNROLL = 4


def _combine_kernel(idx_ref, ts_ref, x_ref, gate_ref, y_hbm, o_ref, ybuf_ref, sem, be_ref, bk_ref, acc_ref,
                    *, tc, cap, n_e):
    b = pl.program_id(0)
    c = pl.program_id(1)
    g = pl.program_id(2)
    eg, br = COMBINE_EG, COMBINE_BR
    tiles = tc // LANES
    base = c * tc
    nl = x_ref.shape[1] // LANES

    @pl.when(g == 0)
    def _():
        acc_ref[...] = jnp.zeros_like(acc_ref)

    def slot_range(el):
        return ts_ref[el * LANES + c * tiles], ts_ref[el * LANES + (c + 1) * tiles]

    def build(el, n):
        lo, hi = slot_range(el)
        k0 = lo // br
        k1 = jnp.where(hi > lo, (hi + br - 1) // br, k0)

        def push(k, m):
            be_ref[m] = el
            bk_ref[m] = k
            return m + 1

        return lax.fori_loop(k0, k1, push, n)

    nblk = lax.fori_loop(0, eg, build, 0)

    def blk_copy(i, slot):
        row = b * n_e + g * eg + be_ref[i]
        return pltpu.make_async_copy(y_hbm.at[row, pl.ds(bk_ref[i] * (br * nl), br * nl), :], ybuf_ref.at[slot],
                                     sem.at[slot])

    ahead = COMBINE_NBUF - 1
    for i0 in range(ahead):
        @pl.when(i0 < nblk)
        def _():
            blk_copy(i0, i0).start()

    def per_block(i, carry):
        slot = i % COMBINE_NBUF

        @pl.when(i + ahead < nblk)
        def _():
            blk_copy(i + ahead, (i + ahead) % COMBINE_NBUF).start()

        blk_copy(i, slot).wait()
        el = be_ref[i]
        r0 = bk_ref[i] * br
        lo, hi = slot_range(el)
        j0 = jnp.maximum(lo, r0)
        j1 = jnp.minimum(hi, r0 + br)

        def add_rows(j, n):
            offs = [pl.multiple_of((idx_ref[el * cap + j + r] - base) * nl, nl) for r in range(n)]
            vals = [acc_ref[pl.ds(offs[r], nl), :]
                    + ybuf_ref[slot, pl.ds(pl.multiple_of((j + r - r0) * nl, nl), nl), :] for r in range(n)]
            for r in range(n):
                acc_ref[pl.ds(offs[r], nl), :] = vals[r]

        ngrp = (j1 - j0) // COMBINE_UNROLL

        def grp(t, cc):
            add_rows(j0 + t * COMBINE_UNROLL, COMBINE_UNROLL)
            return cc

        lax.fori_loop(0, ngrp, grp, 0)

        def one(j, cc):
            add_rows(j, 1)
            return cc

        lax.fori_loop(j0 + ngrp * COMBINE_UNROLL, j1, one, 0)
        return carry

    lax.fori_loop(0, nblk, per_block, 0)

    @pl.when(g == n_e // eg - 1)
    def _():
        for t in range(nl):
            ls = slice(t * LANES, (t + 1) * LANES)
            o_ref[:, ls] = x_ref[:, ls] + gate_ref[:, ls] * acc_ref[pl.ds(t, tc, stride=nl), :]


def combine(idx, tstart, x, gate, y, tc=2048):
    B, S, D = x.shape
    _, E, cap = idx.shape
    tc = min(tc, S)
    eg = COMBINE_EG
    ng = E // eg
    max_blocks = eg * (cap // COMBINE_BR + 1)
    return pl.pallas_call(
        functools.partial(_combine_kernel, tc=tc, cap=cap, n_e=E),
        grid=(B, S // tc, ng),
        in_specs=[
            pl.BlockSpec((eg * cap,), lambda b, c, g: (b * ng + g,), memory_space=pltpu.SMEM),
            pl.BlockSpec((eg * LANES,), lambda b, c, g: (b * ng + g,), memory_space=pltpu.SMEM),
            pl.BlockSpec((None, tc, D), lambda b, c, g: (b, c, 0)),
            pl.BlockSpec((None, 1, D), lambda b, c, g: (b, 0, 0)),
            pl.BlockSpec(memory_space=pl.ANY),
        ],
        out_specs=pl.BlockSpec((None, tc, D), lambda b, c, g: (b, c, 0)),
        out_shape=jax.ShapeDtypeStruct((B, S, D), F32),
        scratch_shapes=[
            pltpu.VMEM((COMBINE_NBUF, COMBINE_BR * D // LANES, LANES), F32),
            pltpu.SemaphoreType.DMA((COMBINE_NBUF,)),
            pltpu.SMEM((max_blocks,), I32),
            pltpu.SMEM((max_blocks,), I32),
            pltpu.VMEM((tc * D // LANES, LANES), F32),
        ],
        compiler_params=_cparams(("arbitrary", "arbitrary", "arbitrary")),
        name="combine",
    )(idx.reshape(-1), tstart.reshape(-1), x, gate, y)


def _final_kernel(x_ref, g_ref, o_ref):
    x = x_ref[...]
    o_ref[...] = x * lax.rsqrt(jnp.mean(x * x, axis=-1, keepdims=True) + EPS) * g_ref[...]


def final_norm(x, g, tm=1024):
    B, S, D = x.shape
    return pl.pallas_call(
        _final_kernel,
        grid=(B, S // tm),
        in_specs=[pl.BlockSpec((None, tm, D), lambda b, i: (b, i, 0)), pl.BlockSpec((1, D), lambda b, i: (0, 0))],
        out_specs=pl.BlockSpec((None, tm, D), lambda b, i: (b, i, 0)),
        out_shape=jax.ShapeDtypeStruct((B, S, D), F32),
        compiler_params=_cparams(("arbitrary", "arbitrary")),
        name="final_norm",
    )(x, g)


def _pad_lanes(a):
    return jnp.pad(a, ((0, 0), (0, LANES - a.shape[-1])))


def moe_block(x, h, aff_t, gate, wg, wu, wd, layer):
    idx, g, tstart = select_tokens(aff_t)
    xg = gather_rows(idx, h, wg.shape[-2] // 2 // LANES)
    y = expert_ffn(g, xg, wg, wu, wd, layer)
    return combine(idx, tstart, x, gate, y)


def kernel(x, c, positions, ada_w, ada_b, mix_norm_g, ffn_norm_g, fc_w_in, conv_w, conv_b, conv_ln_g, conv_ln_b, fc_w_out, attn_w_qkv, attn_sink, attn_w_out, router_w, router_b, moe_w_gate, moe_w_up, moe_w_down, final_norm_g):
    B, S, D = x.shape
    depth = ada_w.shape[0]
    mod = adaln_all(c, ada_w.reshape(depth * 2, D, 3 * D), ada_b.reshape(depth * 2, 1, 3 * D))
    mod = mod.reshape(depth, 2, B, 1, 3 * D)
    cos, sin = rope_tables(positions)
    for l in range(depth):
        i = l // 2
        shift, scale, gate = (mod[l, 0, :, :, k * D:(k + 1) * D] for k in range(3))
        shift2, scale2, gate2 = (mod[l, 1, :, :, k * D:(k + 1) * D] for k in range(3))
        g1 = mix_norm_g[l][None, :]
        if l % 2 == 0:
            uf, glu = even_in(x, g1, scale, shift, fc_w_in[i].astype(BF16))
            yf = fourier_mix(uf)
            yc = conv_ln_swish(glu, conv_w[i], conv_b[i][None, :], conv_ln_g[i][None, :], conv_ln_b[i][None, :])
            wo = fc_w_out[i].astype(BF16)
            ys, ws = [yf, yc], [wo[:FOURIER_WIDTH], wo[FOURIER_WIDTH:]]
        else:
            q, k, v = odd_in(x, g1, scale, shift, attn_w_qkv[i].astype(BF16), cos, sin)
            ys, ws = [attention(q, k, v, attn_sink[i])], [attn_w_out[i].astype(BF16)]
        rw = _pad_lanes(router_w[l])
        rwh = rw.astype(BF16)
        rwl = (rw - rwh.astype(F32)).astype(BF16)
        x, h, aff_t = post_mixer(ys, ws, x, gate, ffn_norm_g[l][None, :], scale2, shift2,
                                 rwh, rwl, _pad_lanes(router_b[l][None, :]))
        x = moe_block(x, h, aff_t, gate2, moe_w_gate, moe_w_up, moe_w_down, l)
    return final_norm(x, final_norm_g[None, :])
```

```python
import functools

import numpy as np
import jax
import jax.numpy as jnp
from jax import lax
from jax.experimental import pallas as pl
from jax.experimental.pallas import tpu as pltpu

F32 = jnp.float32
BF16 = jnp.bfloat16
I32 = jnp.int32

EPS = 1e-6
HEAD_DIM = 64
N_Q_HEADS = 16
N_KV_HEADS = 4
GQA = N_Q_HEADS // N_KV_HEADS
BLOCK = 128
N_EXPERTS = 16
CAPACITY_FACTOR = 2
CONV_KERNEL = 31
FOURIER_WIDTH = 512
CONV_WIDTH = 512
GROUP_CH = 128
ROPE_THETA = 10000.0
LANES = 128
SUBLANES = 8
FFT_N2 = 256
HALO = 16
VMEM_LIMIT = 56 * 1024 * 1024


def _cparams(sem):
    return pltpu.CompilerParams(dimension_semantics=sem, vmem_limit_bytes=VMEM_LIMIT)


def _rms_mod(x, g, scale, shift):
    y = x * lax.rsqrt(jnp.mean(x * x, axis=-1, keepdims=True) + EPS)
    return (y * g) * (1.0 + scale) + shift


def _adaln_kernel(ct_ref, w_ref, b_ref, o_ref):
    ct = ct_ref[...]
    cond = ct * jax.nn.sigmoid(ct)
    w = w_ref[...]
    rows = [jnp.sum(w * cond[:, b:b + 1], axis=0, keepdims=True) for b in range(ct.shape[1])]
    o_ref[...] = jnp.concatenate(rows, axis=0) + b_ref[...]


def adaln_all(c, ada_w, ada_b):
    B, D = c.shape
    L, _, N = ada_w.shape
    tn = 1024
    return pl.pallas_call(
        _adaln_kernel,
        grid=(L, N // tn),
        in_specs=[
            pl.BlockSpec((D, B), lambda l, j: (0, 0)),
            pl.BlockSpec((None, D, tn), lambda l, j: (l, 0, j)),
            pl.BlockSpec((None, 1, tn), lambda l, j: (l, 0, j)),
        ],
        out_specs=pl.BlockSpec((None, B, tn), lambda l, j: (l, 0, j)),
        out_shape=jax.ShapeDtypeStruct((L, B, N), F32),
        compiler_params=_cparams(("arbitrary", "arbitrary")),
        name="adaln",
    )(c.T, ada_w, ada_b)


def _even_in_kernel(x_ref, g_ref, sc_ref, sh_ref, w_ref, uf_ref, glu_ref):
    h = _rms_mod(x_ref[...], g_ref[...], sc_ref[...], sh_ref[...])
    p = jnp.dot(h.astype(BF16), w_ref[...], preferred_element_type=F32)
    fw, cw = FOURIER_WIDTH, CONV_WIDTH
    uf_ref[...] = p[:, :fw].astype(BF16)
    glu_ref[...] = p[:, fw:fw + cw] * jax.nn.sigmoid(p[:, fw + cw:])


def even_in(x, g, scale, shift, w, tm=512):
    B, S, D = x.shape
    N = w.shape[1]
    return pl.pallas_call(
        _even_in_kernel,
        grid=(B, S // tm),
        in_specs=[
            pl.BlockSpec((None, tm, D), lambda b, i: (b, i, 0)),
            pl.BlockSpec((1, D), lambda b, i: (0, 0)),
            pl.BlockSpec((None, 1, D), lambda b, i: (b, 0, 0)),
            pl.BlockSpec((None, 1, D), lambda b, i: (b, 0, 0)),
            pl.BlockSpec((D, N), lambda b, i: (0, 0)),
        ],
        out_specs=[
            pl.BlockSpec((None, tm, FOURIER_WIDTH), lambda b, i: (b, i, 0)),
            pl.BlockSpec((None, tm, CONV_WIDTH), lambda b, i: (b, i, 0)),
        ],
        out_shape=[
            jax.ShapeDtypeStruct((B, S, FOURIER_WIDTH), BF16),
            jax.ShapeDtypeStruct((B, S, CONV_WIDTH), F32),
        ],
        compiler_params=_cparams(("arbitrary", "arbitrary")),
        name="even_in",
    )(x, g, scale, shift, w)


def _fourier_consts(S):
    n2 = FFT_N2
    n1 = S // n2
    C = GROUP_CH
    scale = 1.0 / np.sqrt(float(S) * C)
    cm = 2.0 * np.pi * np.outer(np.arange(C), np.arange(C)) / C
    ccs = np.concatenate([np.cos(cm), -np.sin(cm)], axis=1) * scale
    ac = 2.0 * np.pi * np.outer(np.arange(n1), np.arange(n1)) / n1
    eye = np.eye(SUBLANES)
    kcat = np.concatenate([np.kron(np.cos(ac), eye), np.kron(-np.sin(ac), eye)], axis=0)
    cc = np.repeat(np.arange(n1), SUBLANES)[:, None]
    ii = np.tile(np.arange(SUBLANES), n1)[:, None]
    jj = np.arange(n2 // SUBLANES)[None, :]
    ang = 2.0 * np.pi * (SUBLANES * jj + ii) * cc / S
    twr, twi = np.cos(ang), -np.sin(ang)
    bd = 2.0 * np.pi * np.outer(np.arange(n2), np.arange(n2)) / n2
    w2 = np.concatenate([np.cos(bd), np.sin(bd)], axis=1)
    f = lambda a: jnp.asarray(a, F32)
    return f(ccs).astype(BF16), f(kcat).astype(BF16), f(twr), f(twi), f(w2).astype(BF16)


def _fourier_kernel(u_ref, ccs_ref, kcat_ref, twr_ref, twi_ref, w2_ref, o_ref, z_ref, y_ref, *, S):
    n2 = FFT_N2
    n1 = S // n2
    R = SUBLANES * n1
    C = GROUP_CH
    ch = min(S, 512)
    for r in range(S // ch):
        z_ref[r * ch:(r + 1) * ch, :] = jnp.dot(u_ref[r * ch:(r + 1) * ch, :], ccs_ref[...],
                                                preferred_element_type=F32)
    kcat = kcat_ref[...]
    for j in range(n2 // SUBLANES):
        r0 = SUBLANES * j
        d = jnp.concatenate([z_ref[a * n2 + r0:a * n2 + r0 + SUBLANES, :] for a in range(n1)], axis=0)
        p = jnp.dot(kcat, d.astype(BF16), preferred_element_type=F32)
        o_r = p[:R, :C] - p[R:, C:]
        o_i = p[R:, :C] + p[:R, C:]
        tr = twr_ref[:, j:j + 1]
        ti = twi_ref[:, j:j + 1]
        a_r = o_r * tr - o_i * ti
        a_i = o_r * ti + o_i * tr
        for c in range(n1):
            z_ref[c * n2 + r0:c * n2 + r0 + SUBLANES, :C] = a_r[SUBLANES * c:SUBLANES * (c + 1), :]
            z_ref[c * n2 + r0:c * n2 + r0 + SUBLANES, C:] = a_i[SUBLANES * c:SUBLANES * (c + 1), :]
    w2 = w2_ref[...]
    for c in range(n1):
        a = z_ref[c * n2:(c + 1) * n2, :]
        rhs = jnp.concatenate([a[:, :C], a[:, C:]], axis=0).astype(BF16)
        y_ref[pl.ds(c, n2, stride=n1), :] = jnp.dot(w2, rhs, preferred_element_type=F32)
    o_ref[...] = y_ref[...].astype(BF16)


def fourier_mix(uf):
    B, S, W = uf.shape
    consts = _fourier_consts(S)
    const_specs = [pl.BlockSpec(c.shape, lambda b, g: (0, 0)) for c in consts]
    return pl.pallas_call(
        functools.partial(_fourier_kernel, S=S),
        grid=(B, W // GROUP_CH),
        in_specs=[pl.BlockSpec((None, S, GROUP_CH), lambda b, g: (b, 0, g))] + const_specs,
        out_specs=pl.BlockSpec((None, S, GROUP_CH), lambda b, g: (b, 0, g)),
        out_shape=jax.ShapeDtypeStruct((B, S, W), BF16),
        scratch_shapes=[pltpu.VMEM((S, 2 * GROUP_CH), F32), pltpu.VMEM((S, GROUP_CH), F32)],
        compiler_params=_cparams(("arbitrary", "arbitrary")),
        name="fourier",
    )(uf, *consts)


def _conv_kernel(prev_ref, cur_ref, next_ref, w_ref, b_ref, g_ref, beta_ref, o_ref, ext_ref, sh_ref, *, tm, nt):
    i = pl.program_id(1)
    ext_ref[0:HALO, :] = jnp.where(i > 0, prev_ref[...], 0.0)
    ext_ref[HALO:HALO + tm, :] = cur_ref[...]
    ext_ref[HALO + tm:2 * HALO + tm, :] = jnp.where(i < nt - 1, next_ref[...], 0.0)
    pad = CONV_KERNEL // 2
    span = tm + 2 * HALO - SUBLANES
    for r in range(1, SUBLANES):
        sh_ref[r, 0:span, :] = ext_ref[r:r + span, :]
    acc = None
    for t in range(CONV_KERNEL):
        o = HALO - pad + t
        q, r = o // SUBLANES * SUBLANES, o % SUBLANES
        rows = ext_ref[q:q + tm, :] if r == 0 else sh_ref[r, q:q + tm, :]
        term = rows * w_ref[t:t + 1, :]
        acc = term if acc is None else acc + term
    conv = acc + b_ref[...]
    mu = jnp.mean(conv, axis=-1, keepdims=True)
    var = jnp.mean(jnp.square(conv - mu), axis=-1, keepdims=True)
    z = (conv - mu) * lax.rsqrt(var + EPS) * g_ref[...] + beta_ref[...]
    o_ref[...] = (z * jax.nn.sigmoid(z)).astype(BF16)


def conv_ln_swish(glu, w, b, g, beta, tm=512):
    B, S, C = glu.shape
    nt = S // tm
    hb = tm // HALO
    nh = S // HALO
    return pl.pallas_call(
        functools.partial(_conv_kernel, tm=tm, nt=nt),
        grid=(B, nt),
        in_specs=[
            pl.BlockSpec((None, HALO, C), lambda bb, i: (bb, jnp.maximum(i * hb - 1, 0), 0)),
            pl.BlockSpec((None, tm, C), lambda bb, i: (bb, i, 0)),
            pl.BlockSpec((None, HALO, C), lambda bb, i: (bb, jnp.minimum((i + 1) * hb, nh - 1), 0)),
            pl.BlockSpec((CONV_KERNEL, C), lambda bb, i: (0, 0)),
            pl.BlockSpec((1, C), lambda bb, i: (0, 0)),
            pl.BlockSpec((1, C), lambda bb, i: (0, 0)),
            pl.BlockSpec((1, C), lambda bb, i: (0, 0)),
        ],
        out_specs=pl.BlockSpec((None, tm, C), lambda bb, i: (bb, i, 0)),
        out_shape=jax.ShapeDtypeStruct((B, S, C), BF16),
        scratch_shapes=[pltpu.VMEM((tm + 2 * HALO, C), F32), pltpu.VMEM((SUBLANES, tm + 2 * HALO, C), F32)],
        compiler_params=_cparams(("arbitrary", "arbitrary")),
        name="conv",
    )(glu, glu, glu, w, b, g, beta)


def _rope_tab_kernel(pos_ref, inv_ref, sgn_ref, cos_ref, sin_ref):
    ang = pos_ref[...].astype(F32) * inv_ref[...]
    cos_ref[...] = jnp.cos(ang)
    sin_ref[...] = jnp.sin(ang) * sgn_ref[...]


def rope_tables(positions, ts=1024):
    B, S = positions.shape
    half = HEAD_DIM // 2
    inv = ROPE_THETA ** (-jnp.arange(0, HEAD_DIM, 2, dtype=F32) / HEAD_DIM)
    reps = LANES // half
    inv_row = jnp.tile(inv, reps)[None, :]
    sgn_row = jnp.tile(jnp.concatenate([-jnp.ones((half,), F32), jnp.ones((half,), F32)]), reps // 2)[None, :]
    return pl.pallas_call(
        _rope_tab_kernel,
        grid=(B, S // ts),
        in_specs=[
            pl.BlockSpec((None, ts, 1), lambda b, i: (b, i, 0)),
            pl.BlockSpec((1, LANES), lambda b, i: (0, 0)),
            pl.BlockSpec((1, LANES), lambda b, i: (0, 0)),
        ],
        out_specs=[pl.BlockSpec((None, ts, LANES), lambda b, i: (b, i, 0))] * 2,
        out_shape=[jax.ShapeDtypeStruct((B, S, LANES), F32)] * 2,
        compiler_params=_cparams(("arbitrary", "arbitrary")),
        name="rope_tab",
    )(positions.reshape(B, S, 1), inv_row, sgn_row)


def _rope_block(t, cos, sin_signed, first_half):
    half = HEAD_DIM // 2
    rot = jnp.where(first_half, pltpu.roll(t, LANES - half, 1), pltpu.roll(t, half, 1))
    return t * cos + rot * sin_signed


def _odd_in_kernel(x_ref, g_ref, sc_ref, sh_ref, w_ref, cos_ref, sin_ref, q_ref, k_ref, v_ref):
    h = _rms_mod(x_ref[...], g_ref[...], sc_ref[...], sh_ref[...])
    p = jnp.dot(h.astype(BF16), w_ref[...], preferred_element_type=F32)
    qd = N_Q_HEADS * HEAD_DIM
    kd = N_KV_HEADS * HEAD_DIM
    cos = cos_ref[...]
    sin = sin_ref[...]
    lane = lax.broadcasted_iota(I32, cos.shape, 1)
    first_half = (lane % HEAD_DIM) < (HEAD_DIM // 2)
    qscale = HEAD_DIM ** -0.5
    for j in range(qd // LANES):
        blk = _rope_block(p[:, j * LANES:(j + 1) * LANES], cos, sin, first_half)
        q_ref[:, j * LANES:(j + 1) * LANES] = (blk * qscale).astype(BF16)
    for j in range(kd // LANES):
        blk = _rope_block(p[:, qd + j * LANES:qd + (j + 1) * LANES], cos, sin, first_half)
        k_ref[:, j * LANES:(j + 1) * LANES] = blk.astype(BF16)
    v_ref[...] = p[:, qd + kd:].astype(BF16)


def odd_in(x, g, scale, shift, w, cos, sin, tm=512):
    B, S, D = x.shape
    N = w.shape[1]
    qd = N_Q_HEADS * HEAD_DIM
    kd = N_KV_HEADS * HEAD_DIM
    return pl.pallas_call(
        _odd_in_kernel,
        grid=(B, S // tm),
        in_specs=[
            pl.BlockSpec((None, tm, D), lambda b, i: (b, i, 0)),
            pl.BlockSpec((1, D), lambda b, i: (0, 0)),
            pl.BlockSpec((None, 1, D), lambda b, i: (b, 0, 0)),
            pl.BlockSpec((None, 1, D), lambda b, i: (b, 0, 0)),
            pl.BlockSpec((D, N), lambda b, i: (0, 0)),
            pl.BlockSpec((None, tm, LANES), lambda b, i: (b, i, 0)),
            pl.BlockSpec((None, tm, LANES), lambda b, i: (b, i, 0)),
        ],
        out_specs=[
            pl.BlockSpec((None, tm, qd), lambda b, i: (b, i, 0)),
            pl.BlockSpec((None, tm, kd), lambda b, i: (b, i, 0)),
            pl.BlockSpec((None, tm, kd), lambda b, i: (b, i, 0)),
        ],
        out_shape=[
            jax.ShapeDtypeStruct((B, S, qd), BF16),
            jax.ShapeDtypeStruct((B, S, kd), BF16),
            jax.ShapeDtypeStruct((B, S, kd), BF16),
        ],
        compiler_params=_cparams(("arbitrary", "arbitrary")),
        name="odd_in",
    )(x, g, scale, shift, w, cos, sin)


def _attn_kernel(sink_ref, q_ref, kp_ref, kc_ref, kn_ref, vp_ref, vc_ref, vn_ref, o_ref, *, nb):
    n = pl.program_id(1)
    T = BLOCK
    kwin = jnp.concatenate([kp_ref[...], kc_ref[...], kn_ref[...]], axis=0)
    vwin = jnp.concatenate([vp_ref[...], vc_ref[...], vn_ref[...]], axis=0)
    qi = lax.broadcasted_iota(I32, (T, 3 * T), 0)
    kj = lax.broadcasted_iota(I32, (T, 3 * T), 1)
    rel = kj - T - qi
    kpos = (n - 1) * T + kj
    valid1 = (jnp.abs(rel) <= BLOCK) & (kpos >= 0) & (kpos < nb * T)
    valid = jnp.concatenate([valid1] * GQA, axis=0)
    grp = lax.broadcasted_iota(I32, (T, GQA * HEAD_DIM), 1) // HEAD_DIM
    left = lax.broadcasted_iota(I32, (T, LANES), 1) < HEAD_DIM
    ones = jnp.ones((3 * T, LANES), BF16)
    for kh in range(N_KV_HEADS):
        kk = kwin[:, kh * HEAD_DIM:(kh + 1) * HEAD_DIM]
        vv = vwin[:, kh * HEAD_DIM:(kh + 1) * HEAD_DIM]
        k4 = jnp.concatenate([kk] * GQA, axis=1)
        v1 = jnp.concatenate([vv, vv, ones], axis=1)
        qh = q_ref[:, kh * GQA * HEAD_DIM:(kh + 1) * GQA * HEAD_DIM]
        qm = jnp.concatenate([jnp.where(grp == g, qh, jnp.zeros_like(qh)) for g in range(GQA)], axis=0)
        s = lax.dot_general(qm, k4, (((1,), (1,)), ((), ())), preferred_element_type=F32)
        s = jnp.where(valid, s, -jnp.inf)
        sink = jnp.concatenate(
            [jnp.full((T, 1), sink_ref[kh * GQA + g], F32) for g in range(GQA)], axis=0)
        m = jnp.maximum(jnp.max(s, axis=-1, keepdims=True), sink)
        e = jnp.exp(s - m).astype(BF16)
        od = jnp.dot(e, v1, preferred_element_type=F32)
        o = od[:, :LANES] / (od[:, LANES:] + jnp.exp(sink - m))
        for t in range(GQA // 2):
            tile = jnp.where(left, o[2 * t * T:(2 * t + 1) * T], o[(2 * t + 1) * T:(2 * t + 2) * T])
            c0 = kh * GQA * HEAD_DIM + t * LANES
            o_ref[:, c0:c0 + LANES] = tile.astype(BF16)


def attention(q, k, v, sink):
    B, S, qd = q.shape
    kd = k.shape[-1]
    nb = S // BLOCK
    prev = lambda b, n: (b, jnp.maximum(n - 1, 0), 0)
    cur = lambda b, n: (b, n, 0)
    nxt = lambda b, n: (b, jnp.minimum(n + 1, nb - 1), 0)
    kv = lambda im: pl.BlockSpec((None, BLOCK, kd), im)
    return pl.pallas_call(
        functools.partial(_attn_kernel, nb=nb),
        grid=(B, nb),
        in_specs=[
            pl.BlockSpec(memory_space=pltpu.SMEM),
            pl.BlockSpec((None, BLOCK, qd), cur),
            kv(prev), kv(cur), kv(nxt), kv(prev), kv(cur), kv(nxt),
        ],
        out_specs=pl.BlockSpec((None, BLOCK, qd), cur),
        out_shape=jax.ShapeDtypeStruct((B, S, qd), BF16),
        compiler_params=_cparams(("arbitrary", "arbitrary")),
        name="attn",
    )(sink, q, k, k, k, v, v, v)


def _post_kernel(*refs, n_y):
    y_refs = refs[:n_y]
    w_refs = refs[n_y:2 * n_y]
    x_ref, gate_ref, g2_ref, sc2_ref, sh2_ref, rwh_ref, rwl_ref, rb_ref = refs[2 * n_y:2 * n_y + 8]
    xo_ref, h_ref, aff_ref = refs[2 * n_y + 8:]
    acc = jnp.dot(y_refs[0][...], w_refs[0][...], preferred_element_type=F32)
    for yr, wr in zip(y_refs[1:], w_refs[1:]):
        acc = acc + jnp.dot(yr[...], wr[...], preferred_element_type=F32)
    xn = x_ref[...] + gate_ref[...] * acc
    xo_ref[...] = xn
    h = _rms_mod(xn, g2_ref[...], sc2_ref[...], sh2_ref[...])
    hh = h.astype(BF16)
    half = h.shape[1] // 2
    lo = pltpu.bitcast(hh[:, :half].astype(F32), jnp.uint32) >> 16
    hi = pltpu.bitcast(hh[:, half:].astype(F32), jnp.uint32) & jnp.uint32(0xFFFF0000)
    packed = lo | hi
    nl = half // LANES
    for t in range(nl):
        h_ref[pl.ds(t, h.shape[0], stride=nl), :] = packed[:, t * LANES:(t + 1) * LANES]
    hl = (h - hh.astype(F32)).astype(BF16)
    rwh = rwh_ref[...]
    logits = (jnp.dot(hh, rwh, preferred_element_type=F32)
              + jnp.dot(hl, rwh, preferred_element_type=F32)
              + jnp.dot(hh, rwl_ref[...], preferred_element_type=F32)) + rb_ref[...]
    lane = lax.broadcasted_iota(I32, logits.shape, 1)
    logits = jnp.where(lane < N_EXPERTS, logits, -jnp.inf)
    m = jnp.max(logits, axis=-1, keepdims=True)
    e = jnp.exp(logits - m)
    aff = e / jnp.sum(e, axis=-1, keepdims=True)
    aff_ref[...] = aff.T[:N_EXPERTS, :]


def post_mixer(ys, ws, x, gate, g2, sc2, sh2, rwh, rwl, rb, tm=512):
    B, S, D = x.shape
    n_y = len(ys)
    in_specs = [pl.BlockSpec((None, tm, y.shape[-1]), lambda b, i: (b, i, 0)) for y in ys]
    in_specs += [pl.BlockSpec(w.shape, lambda b, i: (0, 0)) for w in ws]
    in_specs += [
        pl.BlockSpec((None, tm, D), lambda b, i: (b, i, 0)),
        pl.BlockSpec((None, 1, D), lambda b, i: (b, 0, 0)),
        pl.BlockSpec((1, D), lambda b, i: (0, 0)),
        pl.BlockSpec((None, 1, D), lambda b, i: (b, 0, 0)),
        pl.BlockSpec((None, 1, D), lambda b, i: (b, 0, 0)),
        pl.BlockSpec((D, LANES), lambda b, i: (0, 0)),
        pl.BlockSpec((D, LANES), lambda b, i: (0, 0)),
        pl.BlockSpec((1, LANES), lambda b, i: (0, 0)),
    ]
    return pl.pallas_call(
        functools.partial(_post_kernel, n_y=n_y),
        grid=(B, S // tm),
        in_specs=in_specs,
        out_specs=[
            pl.BlockSpec((None, tm, D), lambda b, i: (b, i, 0)),
            pl.BlockSpec((None, tm * (D // 2) // LANES, LANES), lambda b, i: (b, i, 0)),
            pl.BlockSpec((None, N_EXPERTS, tm), lambda b, i: (b, 0, i)),
        ],
        out_shape=[
            jax.ShapeDtypeStruct((B, S, D), F32),
            jax.ShapeDtypeStruct((B, S * (D // 2) // LANES, LANES), jnp.uint32),
            jax.ShapeDtypeStruct((B, N_EXPERTS, S), F32),
        ],
        compiler_params=_cparams(("arbitrary", "arbitrary")),
        name="post",
    )(*ys, *ws, x, gate, g2, sc2, sh2, rwh, rwl, rb)


def _select_consts(S):
    nt = S // LANES
    E = N_EXPERTS
    li = np.arange(LANES)
    u_incl = (li[:, None] <= li[None, :]).astype(np.float32)
    u_strict = (li[:, None] < li[None, :]).astype(np.float32)
    r = np.arange(nt * E)
    tt, ee = r // E, r % E
    l_strict = ((ee[:, None] == ee[None, :]) & (tt[None, :] < tt[:, None])).astype(np.float32)
    tind = (np.arange(S)[:, None] // LANES == li[None, :]).astype(np.float32)
    f = lambda a: jnp.asarray(a, F32).astype(BF16)
    return f(u_incl), f(u_strict), f(l_strict), f(tind)


def _select_kernel(aff_ref, ui_ref, us_ref, ls_ref, tind_ref, idx_ref, gate_ref, ts_ref,
                   affx_ref, lcs_ref, cex_ref, cin_ref, *, S, cap):
    E = N_EXPERTS
    nt = S // LANES
    aff = aff_ref[...]
    v = pltpu.bitcast(aff, I32)
    capf = jnp.float32(cap)

    def count(mask):
        return jnp.sum(jnp.where(mask, 1.0, 0.0), axis=1, keepdims=True)

    def search(i, t):
        cand = t | jnp.left_shift(jnp.int32(1), 30 - i)
        return jnp.where(count(v >= cand) >= capf, cand, t)

    thr = lax.fori_loop(0, 31, search, jnp.zeros((E, 1), I32))
    gt = v > thr
    eq = v == thr
    need = capf - count(gt)

    def to_tiles(m):
        return jnp.concatenate([m[:, LANES * t:LANES * (t + 1)] for t in range(nt)], axis=0)

    def prefix(mt):
        lcs = jnp.dot(mt.astype(BF16), ui_ref[...], preferred_element_type=F32)
        off = jnp.dot(ls_ref[...], lcs.astype(BF16), preferred_element_type=F32)[:, LANES - 1:LANES]
        return lcs, off

    eqt = to_tiles(jnp.where(eq, 1.0, 0.0))
    lcs_e, off_e = prefix(eqt)
    rank_t = lcs_e + off_e - eqt
    rank = jnp.concatenate([rank_t[E * t:E * (t + 1), :] for t in range(nt)], axis=1)
    sel = gt | (eq & (rank < need))
    self_ = jnp.where(sel, 1.0, 0.0)

    tot_et = jnp.dot(self_.astype(BF16), tind_ref[...], preferred_element_type=F32)
    ts_ref[...] = jnp.dot(tot_et.astype(BF16), us_ref[...], preferred_element_type=F32).astype(I32)

    selt = to_tiles(self_)
    lcs, off = prefix(selt)
    nrow = nt * E
    zeros = jnp.zeros((nrow, LANES), F32)
    pad_rows = LANES * E - nrow
    lcs_ref[0:nrow, :] = lcs
    lcs_ref[nrow:LANES * E, :] = jnp.zeros((pad_rows, LANES), F32)
    affx_ref[0:nrow, :] = to_tiles(aff)
    affx_ref[nrow:LANES * E, :] = jnp.zeros((pad_rows, LANES), F32)
    cex_ref[...] = off + zeros
    cin_ref[...] = off + lcs[:, LANES - 1:LANES] + zeros

    jrow = lax.broadcasted_iota(I32, (1, cap), 1).astype(F32)
    trow = lax.broadcasted_iota(I32, (LANES, 1), 0).astype(F32)

    def per_expert(e, carry):
        lc = lcs_ref[pl.ds(e, LANES, stride=E), :]
        af = affx_ref[pl.ds(e, LANES, stride=E), :]
        cx = cex_ref[pl.ds(e, nt, stride=E), :][:, 0:1]
        ci = cin_ref[pl.ds(e, nt, stride=E), :][:, 0:1]
        le = jnp.where(ci <= jrow, 1.0, 0.0)
        tj = jnp.sum(le, axis=0, keepdims=True)
        r = jrow - jnp.sum(le * (ci - cx), axis=0, keepdims=True)
        onehot = jnp.where(trow == tj, 1.0, 0.0).astype(BF16)
        g = jnp.dot(lc.T.astype(BF16), onehot, preferred_element_type=F32)
        lo = jnp.sum(jnp.where(g <= r, 1.0, 0.0), axis=0, keepdims=True)
        idx_ref[pl.ds(e, 1), :] = (tj * LANES + lo).astype(I32)
        aft = af.T
        a1 = aft.astype(BF16)
        r1 = aft - a1.astype(F32)
        a2 = r1.astype(BF16)
        a3 = (r1 - a2.astype(F32)).astype(BF16)
        ag = (jnp.dot(a1, onehot, preferred_element_type=F32) + jnp.dot(a2, onehot, preferred_element_type=F32)
              + jnp.dot(a3, onehot, preferred_element_type=F32))
        gate_ref[pl.ds(e, 1), :] = jnp.sum(jnp.where(trow == lo, ag, 0.0), axis=0, keepdims=True)
        return carry

    lax.fori_loop(0, E, per_expert, 0)


def select_tokens(aff_t):
    B, E, S = aff_t.shape
    cap = CAPACITY_FACTOR * S // E
    nt = S // LANES
    consts = _select_consts(S)
    const_specs = [pl.BlockSpec(c.shape, lambda b: (0, 0)) for c in consts]
    return pl.pallas_call(
        functools.partial(_select_kernel, S=S, cap=cap),
        grid=(B,),
        in_specs=[pl.BlockSpec((None, E, S), lambda b: (b, 0, 0))] + const_specs,
        out_specs=[
            pl.BlockSpec((None, E, cap), lambda b: (b, 0, 0)),
            pl.BlockSpec((None, E, cap), lambda b: (b, 0, 0)),
            pl.BlockSpec((None, E, LANES), lambda b: (b, 0, 0)),
        ],
        out_shape=[
            jax.ShapeDtypeStruct((B, E, cap), I32),
            jax.ShapeDtypeStruct((B, E, cap), F32),
            jax.ShapeDtypeStruct((B, E, LANES), I32),
        ],
        scratch_shapes=[
            pltpu.VMEM((LANES * E, LANES), F32),
            pltpu.VMEM((LANES * E, LANES), F32),
            pltpu.VMEM((nt * E, LANES), F32),
            pltpu.VMEM((nt * E, LANES), F32),
        ],
        compiler_params=_cparams(("arbitrary",)),
        name="select",
    )(aff_t, *consts)


GATHER_UNROLL = 8
FFN_ROWS = 256


def _gather_kernel(idx_ref, hp_ref, xg_ref, *, cap):
    nl = xg_ref.shape[0] // cap

    def gather(t, c):
        j0 = t * GATHER_UNROLL
        rows = [hp_ref[pl.ds(pl.multiple_of(idx_ref[0, j0 + r] * nl, nl), nl), :] for r in range(GATHER_UNROLL)]
        for r in range(GATHER_UNROLL):
            xg_ref[pl.ds(pl.multiple_of((j0 + r) * nl, nl), nl), :] = rows[r]
        return c

    lax.fori_loop(0, cap // GATHER_UNROLL, gather, 0)


def gather_rows(idx, hp, nl):
    B, E, cap = idx.shape
    hrows = hp.shape[1]
    return pl.pallas_call(
        functools.partial(_gather_kernel, cap=cap),
        grid=(B, E),
        in_specs=[
            pl.BlockSpec((None, 1, cap), lambda b, e: (b * E + e, 0, 0), memory_space=pltpu.SMEM),
            pl.BlockSpec((None, hrows, LANES), lambda b, e: (b, 0, 0), pipeline_mode=pl.Buffered(1)),
        ],
        out_specs=pl.BlockSpec((None, cap * nl, LANES), lambda b, e: (b * E + e, 0, 0)),
        out_shape=jax.ShapeDtypeStruct((B * E, cap * nl, LANES), jnp.uint32),
        compiler_params=_cparams(("arbitrary", "arbitrary")),
        name="gather",
    )(idx.reshape(B * E, 1, cap), hp)


def _ffn_kernel(gate_ref, xg_ref, wg_ref, wu_ref, wd_ref, y_ref, wgb_ref, wub_ref, wdb_ref, *, cap):
    @pl.when(pl.program_id(1) == 0)
    def _():
        wgb_ref[...] = wg_ref[...].astype(BF16)
        wub_ref[...] = wu_ref[...].astype(BF16)
        wdb_ref[...] = wd_ref[...].astype(BF16)

    half = wg_ref.shape[0] // 2
    nl = half // LANES
    ntile = nl
    gcol = jnp.broadcast_to(gate_ref[...], (LANES, cap)).T[:, 0:1]
    rows = min(FFN_ROWS, cap)
    for r in range(cap // rows):
        sl = slice(r * rows, (r + 1) * rows)
        xp = jnp.concatenate([xg_ref[pl.ds(r * rows * nl + t, rows, stride=nl), :] for t in range(nl)],
                             axis=1)
        lo = pltpu.bitcast(xp << 16, F32).astype(BF16)
        hi = pltpu.bitcast(xp & jnp.uint32(0xFFFF0000), F32).astype(BF16)
        a = (jnp.dot(lo, wgb_ref[:half, :], preferred_element_type=F32)
             + jnp.dot(hi, wgb_ref[half:, :], preferred_element_type=F32))
        u = (jnp.dot(lo, wub_ref[:half, :], preferred_element_type=F32)
             + jnp.dot(hi, wub_ref[half:, :], preferred_element_type=F32))
        hid = ((a * jax.nn.sigmoid(a)) * u).astype(BF16)
        y = (jnp.dot(hid, wdb_ref[...], preferred_element_type=F32) * gcol[sl, :]).astype(BF16)
        yp = ((pltpu.bitcast(y[:, :half].astype(F32), jnp.uint32) >> 16)
              | (pltpu.bitcast(y[:, half:].astype(F32), jnp.uint32) & jnp.uint32(0xFFFF0000)))
        for t in range(ntile):
            y_ref[pl.ds(r * rows * ntile + t, rows, stride=ntile), :] = yp[:, t * LANES:(t + 1) * LANES]


def expert_ffn(gate, xg, wg, wu, wd, layer):
    B, E, cap = gate.shape
    D, Fdim = wg.shape[-2:]
    xrows = xg.shape[1]
    return pl.pallas_call(
        functools.partial(_ffn_kernel, cap=cap),
        grid=(E, B),
        in_specs=[
            pl.BlockSpec((None, 1, cap), lambda e, b: (b * E + e, 0, 0)),
            pl.BlockSpec((None, xrows, LANES), lambda e, b: (b * E + e, 0, 0)),
            pl.BlockSpec((None, None, D, Fdim), lambda e, b: (layer, e, 0, 0)),
            pl.BlockSpec((None, None, D, Fdim), lambda e, b: (layer, e, 0, 0)),
            pl.BlockSpec((None, None, Fdim, D), lambda e, b: (layer, e, 0, 0)),
        ],
        out_specs=pl.BlockSpec((None, xrows, LANES), lambda e, b: (b * E + e, 0, 0)),
        out_shape=jax.ShapeDtypeStruct((B * E, xrows, LANES), jnp.uint32),
        scratch_shapes=[pltpu.VMEM((D, Fdim), BF16), pltpu.VMEM((D, Fdim), BF16), pltpu.VMEM((Fdim, D), BF16)],
        compiler_params=_cparams(("arbitrary", "arbitrary")),
        name="ffn",
    )(gate.reshape(B * E, 1, cap), xg, wg, wu, wd)


COMBINE_EG = 8
COMBINE_BR = 128
COMBINE_NBUF = 6
COMBINE_UNROLL = 4


def _combine_kernel(idx_ref, ts_ref, x_ref, gate_ref, fg_ref, y_hbm, o_ref, ybuf_ref, sem, be_ref, bk_ref, acc_ref,
                    *, tc, cap, n_e, final_norm_out):
    b = pl.program_id(0)
    c = pl.program_id(1)
    g = pl.program_id(2)
    eg, br = COMBINE_EG, COMBINE_BR
    tiles = tc // LANES
    base = c * tc
    nl = x_ref.shape[1] // LANES
    ny = nl // 2

    @pl.when(g == 0)
    def _():
        acc_ref[...] = jnp.zeros_like(acc_ref)

    def slot_range(el):
        return ts_ref[el * LANES + c * tiles], ts_ref[el * LANES + (c + 1) * tiles]

    def build(el, n):
        lo, hi = slot_range(el)
        k0 = lo // br
        k1 = jnp.where(hi > lo, (hi + br - 1) // br, k0)

        def push(k, m):
            be_ref[m] = el
            bk_ref[m] = k
            return m + 1

        return lax.fori_loop(k0, k1, push, n)

    nblk = lax.fori_loop(0, eg, build, 0)

    def blk_copy(i, slot):
        row = b * n_e + g * eg + be_ref[i]
        return pltpu.make_async_copy(y_hbm.at[row, pl.ds(bk_ref[i] * (br * ny), br * ny), :], ybuf_ref.at[slot],
                                     sem.at[slot])

    ahead = COMBINE_NBUF - 1
    for i0 in range(ahead):
        @pl.when(i0 < nblk)
        def _():
            blk_copy(i0, i0).start()

    def per_block(i, carry):
        slot = i % COMBINE_NBUF

        @pl.when(i + ahead < nblk)
        def _():
            blk_copy(i + ahead, (i + ahead) % COMBINE_NBUF).start()

        blk_copy(i, slot).wait()
        el = be_ref[i]
        r0 = bk_ref[i] * br
        lo, hi = slot_range(el)
        j0 = jnp.maximum(lo, r0)
        j1 = jnp.minimum(hi, r0 + br)

        def unpack(w):
            return jnp.concatenate([pltpu.bitcast(w << 16, F32),
                                    pltpu.bitcast(w & jnp.uint32(0xFFFF0000), F32)], axis=0)

        def add_rows(j, n):
            offs = [pl.multiple_of((idx_ref[el * cap + j + r] - base) * nl, nl) for r in range(n)]
            vals = [acc_ref[pl.ds(offs[r], nl), :]
                    + unpack(ybuf_ref[slot, pl.ds(pl.multiple_of((j + r - r0) * ny, ny), ny), :]) for r in range(n)]
            for r in range(n):
                acc_ref[pl.ds(offs[r], nl), :] = vals[r]

        ngrp = (j1 - j0) // COMBINE_UNROLL

        def grp(t, cc):
            add_rows(j0 + t * COMBINE_UNROLL, COMBINE_UNROLL)
            return cc

        lax.fori_loop(0, ngrp, grp, 0)

        def one(j, cc):
            add_rows(j, 1)
            return cc

        lax.fori_loop(j0 + ngrp * COMBINE_UNROLL, j1, one, 0)
        return carry

    lax.fori_loop(0, nblk, per_block, 0)

    @pl.when(g == n_e // eg - 1)
    def _():
        ss = jnp.zeros((tc, 1), F32)
        for t in range(nl):
            ls = slice(t * LANES, (t + 1) * LANES)
            xn = x_ref[:, ls] + gate_ref[:, ls] * acc_ref[pl.ds(t, tc, stride=nl), :]
            o_ref[:, ls] = xn
            if final_norm_out:
                ss = ss + jnp.sum(xn * xn, axis=-1, keepdims=True)
        if final_norm_out:
            o_ref[...] = o_ref[...] * lax.rsqrt(ss / (nl * LANES) + EPS) * fg_ref[...]


def combine(idx, tstart, x, gate, y, final_g=None, tc=2048):
    B, S, D = x.shape
    _, E, cap = idx.shape
    tc = min(tc, S)
    eg = COMBINE_EG
    ng = E // eg
    max_blocks = eg * (cap // COMBINE_BR + 1)
    fg = jnp.ones((1, D), F32) if final_g is None else final_g
    return pl.pallas_call(
        functools.partial(_combine_kernel, tc=tc, cap=cap, n_e=E, final_norm_out=final_g is not None),
        grid=(B, S // tc, ng),
        in_specs=[
            pl.BlockSpec((eg * cap,), lambda b, c, g: (b * ng + g,), memory_space=pltpu.SMEM),
            pl.BlockSpec((eg * LANES,), lambda b, c, g: (b * ng + g,), memory_space=pltpu.SMEM),
            pl.BlockSpec((None, tc, D), lambda b, c, g: (b, c, 0)),
            pl.BlockSpec((None, 1, D), lambda b, c, g: (b, 0, 0)),
            pl.BlockSpec((1, D), lambda b, c, g: (0, 0)),
            pl.BlockSpec(memory_space=pl.ANY),
        ],
        out_specs=pl.BlockSpec((None, tc, D), lambda b, c, g: (b, c, 0)),
        out_shape=jax.ShapeDtypeStruct((B, S, D), F32),
        scratch_shapes=[
            pltpu.VMEM((COMBINE_NBUF, COMBINE_BR * D // (2 * LANES), LANES), jnp.uint32),
            pltpu.SemaphoreType.DMA((COMBINE_NBUF,)),
            pltpu.SMEM((max_blocks,), I32),
            pltpu.SMEM((max_blocks,), I32),
            pltpu.VMEM((tc * D // LANES, LANES), F32),
        ],
        compiler_params=_cparams(("arbitrary", "arbitrary", "arbitrary")),
        name="combine",
    )(idx.reshape(-1), tstart.reshape(-1), x, gate, fg, y)


def _pad_lanes(a):
    return jnp.pad(a, ((0, 0), (0, LANES - a.shape[-1])))


def moe_block(x, h, aff_t, gate, wg, wu, wd, layer, final_g=None):
    idx, g, tstart = select_tokens(aff_t)
    xg = gather_rows(idx, h, wg.shape[-2] // 2 // LANES)
    y = expert_ffn(g, xg, wg, wu, wd, layer)
    return combine(idx, tstart, x, gate, y, final_g)


def kernel(x, c, positions, ada_w, ada_b, mix_norm_g, ffn_norm_g, fc_w_in, conv_w, conv_b, conv_ln_g, conv_ln_b, fc_w_out, attn_w_qkv, attn_sink, attn_w_out, router_w, router_b, moe_w_gate, moe_w_up, moe_w_down, final_norm_g):
    B, S, D = x.shape
    depth = ada_w.shape[0]
    mod = adaln_all(c, ada_w.reshape(depth * 2, D, 3 * D), ada_b.reshape(depth * 2, 1, 3 * D))
    mod = mod.reshape(depth, 2, B, 1, 3 * D)
    cos, sin = rope_tables(positions)
    for l in range(depth):
        i = l // 2
        shift, scale, gate = (mod[l, 0, :, :, k * D:(k + 1) * D] for k in range(3))
        shift2, scale2, gate2 = (mod[l, 1, :, :, k * D:(k + 1) * D] for k in range(3))
        g1 = mix_norm_g[l][None, :]
        if l % 2 == 0:
            uf, glu = even_in(x, g1, scale, shift, fc_w_in[i].astype(BF16))
            yf = fourier_mix(uf)
            yc = conv_ln_swish(glu, conv_w[i], conv_b[i][None, :], conv_ln_g[i][None, :], conv_ln_b[i][None, :])
            wo = fc_w_out[i].astype(BF16)
            ys, ws = [yf, yc], [wo[:FOURIER_WIDTH], wo[FOURIER_WIDTH:]]
        else:
            q, k, v = odd_in(x, g1, scale, shift, attn_w_qkv[i].astype(BF16), cos, sin)
            ys, ws = [attention(q, k, v, attn_sink[i])], [attn_w_out[i].astype(BF16)]
        rw = _pad_lanes(router_w[l])
        rwh = rw.astype(BF16)
        rwl = (rw - rwh.astype(F32)).astype(BF16)
        x, h, aff_t = post_mixer(ys, ws, x, gate, ffn_norm_g[l][None, :], scale2, shift2,
                                 rwh, rwl, _pad_lanes(router_b[l][None, :]))
        x = moe_block(x, h, aff_t, gate2, moe_w_gate, moe_w_up, moe_w_down, l,
                      final_norm_g[None, :] if l == depth - 1 else None)
    return x
```

```python
import functools

import numpy as np
import jax
import jax.numpy as jnp
from jax import lax
from jax.experimental import pallas as pl
from jax.experimental.pallas import tpu as pltpu

F32 = jnp.float32
BF16 = jnp.bfloat16
I32 = jnp.int32

EPS = 1e-6
HEAD_DIM = 64
N_Q_HEADS = 16
N_KV_HEADS = 4
GQA = N_Q_HEADS // N_KV_HEADS
BLOCK = 128
N_EXPERTS = 16
CAPACITY_FACTOR = 2
CONV_KERNEL = 31
FOURIER_WIDTH = 512
CONV_WIDTH = 512
GROUP_CH = 128
ROPE_THETA = 10000.0
LANES = 128
SUBLANES = 8
FFT_N2 = 256
HALO = 16
VMEM_LIMIT = 56 * 1024 * 1024


def _cparams(sem):
    return pltpu.CompilerParams(dimension_semantics=sem, vmem_limit_bytes=VMEM_LIMIT)


def _rms_mod(x, g, scale, shift):
    y = x * lax.rsqrt(jnp.mean(x * x, axis=-1, keepdims=True) + EPS)
    return (y * g) * (1.0 + scale) + shift


def _adaln_kernel(ct_ref, w_ref, b_ref, o_ref):
    ct = ct_ref[...]
    cond = ct * jax.nn.sigmoid(ct)
    w = w_ref[...]
    rows = [jnp.sum(w * cond[:, b:b + 1], axis=0, keepdims=True) for b in range(ct.shape[1])]
    o_ref[...] = jnp.concatenate(rows, axis=0) + b_ref[...]


def adaln_all(c, ada_w, ada_b):
    B, D = c.shape
    L, _, N = ada_w.shape
    tn = 1024
    return pl.pallas_call(
        _adaln_kernel,
        grid=(L, N // tn),
        in_specs=[
            pl.BlockSpec((D, B), lambda l, j: (0, 0)),
            pl.BlockSpec((None, D, tn), lambda l, j: (l, 0, j)),
            pl.BlockSpec((None, 1, tn), lambda l, j: (l, 0, j)),
        ],
        out_specs=pl.BlockSpec((None, B, tn), lambda l, j: (l, 0, j)),
        out_shape=jax.ShapeDtypeStruct((L, B, N), F32),
        compiler_params=_cparams(("arbitrary", "arbitrary")),
        name="adaln",
    )(c.T, ada_w, ada_b)


def _even_in_kernel(x_ref, g_ref, sc_ref, sh_ref, w_ref, uf_ref, glu_ref):
    h = _rms_mod(x_ref[...], g_ref[...], sc_ref[...], sh_ref[...])
    p = jnp.dot(h.astype(BF16), w_ref[...], preferred_element_type=F32)
    fw, cw = FOURIER_WIDTH, CONV_WIDTH
    uf_ref[...] = p[:, :fw].astype(BF16)
    glu_ref[...] = p[:, fw:fw + cw] * jax.nn.sigmoid(p[:, fw + cw:])


def even_in(x, g, scale, shift, w, tm=512):
    B, S, D = x.shape
    N = w.shape[1]
    return pl.pallas_call(
        _even_in_kernel,
        grid=(B, S // tm),
        in_specs=[
            pl.BlockSpec((None, tm, D), lambda b, i: (b, i, 0)),
            pl.BlockSpec((1, D), lambda b, i: (0, 0)),
            pl.BlockSpec((None, 1, D), lambda b, i: (b, 0, 0)),
            pl.BlockSpec((None, 1, D), lambda b, i: (b, 0, 0)),
            pl.BlockSpec((D, N), lambda b, i: (0, 0)),
        ],
        out_specs=[
            pl.BlockSpec((None, tm, FOURIER_WIDTH), lambda b, i: (b, i, 0)),
            pl.BlockSpec((None, tm, CONV_WIDTH), lambda b, i: (b, i, 0)),
        ],
        out_shape=[
            jax.ShapeDtypeStruct((B, S, FOURIER_WIDTH), BF16),
            jax.ShapeDtypeStruct((B, S, CONV_WIDTH), F32),
        ],
        compiler_params=_cparams(("arbitrary", "arbitrary")),
        name="even_in",
    )(x, g, scale, shift, w)


def _fourier_consts(S):
    n2 = FFT_N2
    n1 = S // n2
    C = GROUP_CH
    scale = 1.0 / np.sqrt(float(S) * C)
    cm = 2.0 * np.pi * np.outer(np.arange(C), np.arange(C)) / C
    ccs = np.concatenate([np.cos(cm), -np.sin(cm)], axis=1) * scale
    ac = 2.0 * np.pi * np.outer(np.arange(n1), np.arange(n1)) / n1
    eye = np.eye(SUBLANES)
    kcat = np.concatenate([np.kron(np.cos(ac), eye), np.kron(-np.sin(ac), eye)], axis=0)
    cc = np.repeat(np.arange(n1), SUBLANES)[:, None]
    ii = np.tile(np.arange(SUBLANES), n1)[:, None]
    jj = np.arange(n2 // SUBLANES)[None, :]
    ang = 2.0 * np.pi * (SUBLANES * jj + ii) * cc / S
    twr, twi = np.cos(ang), -np.sin(ang)
    bd = 2.0 * np.pi * np.outer(np.arange(n2), np.arange(n2)) / n2
    w2 = np.concatenate([np.cos(bd), np.sin(bd)], axis=1)
    f = lambda a: jnp.asarray(a, F32)
    return f(ccs).astype(BF16), f(kcat).astype(BF16), f(twr), f(twi), f(w2).astype(BF16)


def _fourier_kernel(u_ref, ccs_ref, kcat_ref, twr_ref, twi_ref, w2_ref, o_ref, z_ref, y_ref, *, S):
    n2 = FFT_N2
    n1 = S // n2
    R = SUBLANES * n1
    C = GROUP_CH
    ch = min(S, 512)
    for r in range(S // ch):
        z_ref[r * ch:(r + 1) * ch, :] = jnp.dot(u_ref[r * ch:(r + 1) * ch, :], ccs_ref[...],
                                                preferred_element_type=F32)
    kcat = kcat_ref[...]
    for j in range(n2 // SUBLANES):
        r0 = SUBLANES * j
        d = jnp.concatenate([z_ref[a * n2 + r0:a * n2 + r0 + SUBLANES, :] for a in range(n1)], axis=0)
        p = jnp.dot(kcat, d.astype(BF16), preferred_element_type=F32)
        o_r = p[:R, :C] - p[R:, C:]
        o_i = p[R:, :C] + p[:R, C:]
        tr = twr_ref[:, j:j + 1]
        ti = twi_ref[:, j:j + 1]
        a_r = o_r * tr - o_i * ti
        a_i = o_r * ti + o_i * tr
        for c in range(n1):
            z_ref[c * n2 + r0:c * n2 + r0 + SUBLANES, :C] = a_r[SUBLANES * c:SUBLANES * (c + 1), :]
            z_ref[c * n2 + r0:c * n2 + r0 + SUBLANES, C:] = a_i[SUBLANES * c:SUBLANES * (c + 1), :]
    w2 = w2_ref[...]
    for c in range(n1):
        a = z_ref[c * n2:(c + 1) * n2, :]
        rhs = jnp.concatenate([a[:, :C], a[:, C:]], axis=0).astype(BF16)
        y_ref[pl.ds(c, n2, stride=n1), :] = jnp.dot(w2, rhs, preferred_element_type=F32)
    o_ref[...] = y_ref[...].astype(BF16)


def fourier_mix(uf):
    B, S, W = uf.shape
    consts = _fourier_consts(S)
    const_specs = [pl.BlockSpec(c.shape, lambda b, g: (0, 0)) for c in consts]
    return pl.pallas_call(
        functools.partial(_fourier_kernel, S=S),
        grid=(B, W // GROUP_CH),
        in_specs=[pl.BlockSpec((None, S, GROUP_CH), lambda b, g: (b, 0, g))] + const_specs,
        out_specs=pl.BlockSpec((None, S, GROUP_CH), lambda b, g: (b, 0, g)),
        out_shape=jax.ShapeDtypeStruct((B, S, W), BF16),
        scratch_shapes=[pltpu.VMEM((S, 2 * GROUP_CH), F32), pltpu.VMEM((S, GROUP_CH), F32)],
        compiler_params=_cparams(("arbitrary", "arbitrary")),
        name="fourier",
    )(uf, *consts)


def _conv_kernel(prev_ref, cur_ref, next_ref, w_ref, b_ref, g_ref, beta_ref, o_ref, ext_ref, sh_ref, *, tm, nt):
    i = pl.program_id(1)
    ext_ref[0:HALO, :] = jnp.where(i > 0, prev_ref[...], 0.0)
    ext_ref[HALO:HALO + tm, :] = cur_ref[...]
    ext_ref[HALO + tm:2 * HALO + tm, :] = jnp.where(i < nt - 1, next_ref[...], 0.0)
    pad = CONV_KERNEL // 2
    span = tm + 2 * HALO - SUBLANES
    for r in range(1, SUBLANES):
        sh_ref[r, 0:span, :] = ext_ref[r:r + span, :]
    acc = None
    for t in range(CONV_KERNEL):
        o = HALO - pad + t
        q, r = o // SUBLANES * SUBLANES, o % SUBLANES
        rows = ext_ref[q:q + tm, :] if r == 0 else sh_ref[r, q:q + tm, :]
        term = rows * w_ref[t:t + 1, :]
        acc = term if acc is None else acc + term
    conv = acc + b_ref[...]
    mu = jnp.mean(conv, axis=-1, keepdims=True)
    var = jnp.mean(jnp.square(conv - mu), axis=-1, keepdims=True)
    z = (conv - mu) * lax.rsqrt(var + EPS) * g_ref[...] + beta_ref[...]
    o_ref[...] = (z * jax.nn.sigmoid(z)).astype(BF16)


def conv_ln_swish(glu, w, b, g, beta, tm=512):
    B, S, C = glu.shape
    nt = S // tm
    hb = tm // HALO
    nh = S // HALO
    return pl.pallas_call(
        functools.partial(_conv_kernel, tm=tm, nt=nt),
        grid=(B, nt),
        in_specs=[
            pl.BlockSpec((None, HALO, C), lambda bb, i: (bb, jnp.maximum(i * hb - 1, 0), 0)),
            pl.BlockSpec((None, tm, C), lambda bb, i: (bb, i, 0)),
            pl.BlockSpec((None, HALO, C), lambda bb, i: (bb, jnp.minimum((i + 1) * hb, nh - 1), 0)),
            pl.BlockSpec((CONV_KERNEL, C), lambda bb, i: (0, 0)),
            pl.BlockSpec((1, C), lambda bb, i: (0, 0)),
            pl.BlockSpec((1, C), lambda bb, i: (0, 0)),
            pl.BlockSpec((1, C), lambda bb, i: (0, 0)),
        ],
        out_specs=pl.BlockSpec((None, tm, C), lambda bb, i: (bb, i, 0)),
        out_shape=jax.ShapeDtypeStruct((B, S, C), BF16),
        scratch_shapes=[pltpu.VMEM((tm + 2 * HALO, C), F32), pltpu.VMEM((SUBLANES, tm + 2 * HALO, C), F32)],
        compiler_params=_cparams(("arbitrary", "arbitrary")),
        name="conv",
    )(glu, glu, glu, w, b, g, beta)


def _rope_tab_kernel(pos_ref, inv_ref, sgn_ref, cos_ref, sin_ref):
    ang = pos_ref[...].astype(F32) * inv_ref[...]
    cos_ref[...] = jnp.cos(ang)
    sin_ref[...] = jnp.sin(ang) * sgn_ref[...]


def rope_tables(positions, ts=1024):
    B, S = positions.shape
    half = HEAD_DIM // 2
    inv = ROPE_THETA ** (-jnp.arange(0, HEAD_DIM, 2, dtype=F32) / HEAD_DIM)
    reps = LANES // half
    inv_row = jnp.tile(inv, reps)[None, :]
    sgn_row = jnp.tile(jnp.concatenate([-jnp.ones((half,), F32), jnp.ones((half,), F32)]), reps // 2)[None, :]
    return pl.pallas_call(
        _rope_tab_kernel,
        grid=(B, S // ts),
        in_specs=[
            pl.BlockSpec((None, ts, 1), lambda b, i: (b, i, 0)),
            pl.BlockSpec((1, LANES), lambda b, i: (0, 0)),
            pl.BlockSpec((1, LANES), lambda b, i: (0, 0)),
        ],
        out_specs=[pl.BlockSpec((None, ts, LANES), lambda b, i: (b, i, 0))] * 2,
        out_shape=[jax.ShapeDtypeStruct((B, S, LANES), F32)] * 2,
        compiler_params=_cparams(("arbitrary", "arbitrary")),
        name="rope_tab",
    )(positions.reshape(B, S, 1), inv_row, sgn_row)


def _rope_block(t, cos, sin_signed, first_half):
    half = HEAD_DIM // 2
    rot = jnp.where(first_half, pltpu.roll(t, LANES - half, 1), pltpu.roll(t, half, 1))
    return t * cos + rot * sin_signed


def _odd_in_kernel(x_ref, g_ref, sc_ref, sh_ref, w_ref, cos_ref, sin_ref, q_ref, k_ref, v_ref):
    h = _rms_mod(x_ref[...], g_ref[...], sc_ref[...], sh_ref[...])
    p = jnp.dot(h.astype(BF16), w_ref[...], preferred_element_type=F32)
    qd = N_Q_HEADS * HEAD_DIM
    kd = N_KV_HEADS * HEAD_DIM
    cos = cos_ref[...]
    sin = sin_ref[...]
    lane = lax.broadcasted_iota(I32, cos.shape, 1)
    first_half = (lane % HEAD_DIM) < (HEAD_DIM // 2)
    qscale = HEAD_DIM ** -0.5
    for j in range(qd // LANES):
        blk = _rope_block(p[:, j * LANES:(j + 1) * LANES], cos, sin, first_half)
        q_ref[:, j * LANES:(j + 1) * LANES] = (blk * qscale).astype(BF16)
    for j in range(kd // LANES):
        blk = _rope_block(p[:, qd + j * LANES:qd + (j + 1) * LANES], cos, sin, first_half)
        k_ref[:, j * LANES:(j + 1) * LANES] = blk.astype(BF16)
    v_ref[...] = p[:, qd + kd:].astype(BF16)


def odd_in(x, g, scale, shift, w, cos, sin, tm=512):
    B, S, D = x.shape
    N = w.shape[1]
    qd = N_Q_HEADS * HEAD_DIM
    kd = N_KV_HEADS * HEAD_DIM
    return pl.pallas_call(
        _odd_in_kernel,
        grid=(B, S // tm),
        in_specs=[
            pl.BlockSpec((None, tm, D), lambda b, i: (b, i, 0)),
            pl.BlockSpec((1, D), lambda b, i: (0, 0)),
            pl.BlockSpec((None, 1, D), lambda b, i: (b, 0, 0)),
            pl.BlockSpec((None, 1, D), lambda b, i: (b, 0, 0)),
            pl.BlockSpec((D, N), lambda b, i: (0, 0)),
            pl.BlockSpec((None, tm, LANES), lambda b, i: (b, i, 0)),
            pl.BlockSpec((None, tm, LANES), lambda b, i: (b, i, 0)),
        ],
        out_specs=[
            pl.BlockSpec((None, tm, qd), lambda b, i: (b, i, 0)),
            pl.BlockSpec((None, tm, kd), lambda b, i: (b, i, 0)),
            pl.BlockSpec((None, tm, kd), lambda b, i: (b, i, 0)),
        ],
        out_shape=[
            jax.ShapeDtypeStruct((B, S, qd), BF16),
            jax.ShapeDtypeStruct((B, S, kd), BF16),
            jax.ShapeDtypeStruct((B, S, kd), BF16),
        ],
        compiler_params=_cparams(("arbitrary", "arbitrary")),
        name="odd_in",
    )(x, g, scale, shift, w, cos, sin)


def _attn_kernel(sink_ref, q_ref, kp_ref, kc_ref, kn_ref, vp_ref, vc_ref, vn_ref, o_ref, *, nb):
    n = pl.program_id(1)
    T = BLOCK
    kwin = jnp.concatenate([kp_ref[...], kc_ref[...], kn_ref[...]], axis=0)
    vwin = jnp.concatenate([vp_ref[...], vc_ref[...], vn_ref[...]], axis=0)
    qi = lax.broadcasted_iota(I32, (T, 3 * T), 0)
    kj = lax.broadcasted_iota(I32, (T, 3 * T), 1)
    rel = kj - T - qi
    kpos = (n - 1) * T + kj
    valid1 = (jnp.abs(rel) <= BLOCK) & (kpos >= 0) & (kpos < nb * T)
    valid = jnp.concatenate([valid1] * GQA, axis=0)
    grp = lax.broadcasted_iota(I32, (T, GQA * HEAD_DIM), 1) // HEAD_DIM
    left = lax.broadcasted_iota(I32, (T, LANES), 1) < HEAD_DIM
    ones = jnp.ones((3 * T, LANES), BF16)
    for kh in range(N_KV_HEADS):
        kk = kwin[:, kh * HEAD_DIM:(kh + 1) * HEAD_DIM]
        vv = vwin[:, kh * HEAD_DIM:(kh + 1) * HEAD_DIM]
        k4 = jnp.concatenate([kk] * GQA, axis=1)
        v1 = jnp.concatenate([vv, vv, ones], axis=1)
        qh = q_ref[:, kh * GQA * HEAD_DIM:(kh + 1) * GQA * HEAD_DIM]
        qm = jnp.concatenate([jnp.where(grp == g, qh, jnp.zeros_like(qh)) for g in range(GQA)], axis=0)
        s = lax.dot_general(qm, k4, (((1,), (1,)), ((), ())), preferred_element_type=F32)
        s = jnp.where(valid, s, -jnp.inf)
        sink = jnp.concatenate(
            [jnp.full((T, 1), sink_ref[kh * GQA + g], F32) for g in range(GQA)], axis=0)
        m = jnp.maximum(jnp.max(s, axis=-1, keepdims=True), sink)
        e = jnp.exp(s - m).astype(BF16)
        od = jnp.dot(e, v1, preferred_element_type=F32)
        o = od[:, :LANES] / (od[:, LANES:] + jnp.exp(sink - m))
        for t in range(GQA // 2):
            tile = jnp.where(left, o[2 * t * T:(2 * t + 1) * T], o[(2 * t + 1) * T:(2 * t + 2) * T])
            c0 = kh * GQA * HEAD_DIM + t * LANES
            o_ref[:, c0:c0 + LANES] = tile.astype(BF16)


def attention(q, k, v, sink):
    B, S, qd = q.shape
    kd = k.shape[-1]
    nb = S // BLOCK
    prev = lambda b, n: (b, jnp.maximum(n - 1, 0), 0)
    cur = lambda b, n: (b, n, 0)
    nxt = lambda b, n: (b, jnp.minimum(n + 1, nb - 1), 0)
    kv = lambda im: pl.BlockSpec((None, BLOCK, kd), im)
    return pl.pallas_call(
        functools.partial(_attn_kernel, nb=nb),
        grid=(B, nb),
        in_specs=[
            pl.BlockSpec(memory_space=pltpu.SMEM),
            pl.BlockSpec((None, BLOCK, qd), cur),
            kv(prev), kv(cur), kv(nxt), kv(prev), kv(cur), kv(nxt),
        ],
        out_specs=pl.BlockSpec((None, BLOCK, qd), cur),
        out_shape=jax.ShapeDtypeStruct((B, S, qd), BF16),
        compiler_params=_cparams(("arbitrary", "arbitrary")),
        name="attn",
    )(sink, q, k, k, k, v, v, v)


def _post_kernel(*refs, n_y):
    y_refs = refs[:n_y]
    w_refs = refs[n_y:2 * n_y]
    x_ref, gate_ref, g2_ref, sc2_ref, sh2_ref, rwh_ref, rwl_ref, rb_ref = refs[2 * n_y:2 * n_y + 8]
    xo_ref, h_ref, aff_ref = refs[2 * n_y + 8:]
    acc = jnp.dot(y_refs[0][...], w_refs[0][...], preferred_element_type=F32)
    for yr, wr in zip(y_refs[1:], w_refs[1:]):
        acc = acc + jnp.dot(yr[...], wr[...], preferred_element_type=F32)
    xn = x_ref[...] + gate_ref[...] * acc
    xo_ref[...] = xn
    h = _rms_mod(xn, g2_ref[...], sc2_ref[...], sh2_ref[...])
    hh = h.astype(BF16)
    half = h.shape[1] // 2
    lo = pltpu.bitcast(hh[:, :half].astype(F32), jnp.uint32) >> 16
    hi = pltpu.bitcast(hh[:, half:].astype(F32), jnp.uint32) & jnp.uint32(0xFFFF0000)
    packed = lo | hi
    nl = half // LANES
    for t in range(nl):
        h_ref[pl.ds(t, h.shape[0], stride=nl), :] = packed[:, t * LANES:(t + 1) * LANES]
    hl = (h - hh.astype(F32)).astype(BF16)
    rwh = rwh_ref[...]
    logits = (jnp.dot(hh, rwh, preferred_element_type=F32)
              + jnp.dot(hl, rwh, preferred_element_type=F32)
              + jnp.dot(hh, rwl_ref[...], preferred_element_type=F32)) + rb_ref[...]
    lane = lax.broadcasted_iota(I32, logits.shape, 1)
    logits = jnp.where(lane < N_EXPERTS, logits, -jnp.inf)
    m = jnp.max(logits, axis=-1, keepdims=True)
    e = jnp.exp(logits - m)
    aff = e / jnp.sum(e, axis=-1, keepdims=True)
    aff_ref[...] = aff.T[:N_EXPERTS, :]


def post_mixer(ys, ws, x, gate, g2, sc2, sh2, rwh, rwl, rb, tm=512):
    B, S, D = x.shape
    n_y = len(ys)
    in_specs = [pl.BlockSpec((None, tm, y.shape[-1]), lambda b, i: (b, i, 0)) for y in ys]
    in_specs += [pl.BlockSpec(w.shape, lambda b, i: (0, 0)) for w in ws]
    in_specs += [
        pl.BlockSpec((None, tm, D), lambda b, i: (b, i, 0)),
        pl.BlockSpec((None, 1, D), lambda b, i: (b, 0, 0)),
        pl.BlockSpec((1, D), lambda b, i: (0, 0)),
        pl.BlockSpec((None, 1, D), lambda b, i: (b, 0, 0)),
        pl.BlockSpec((None, 1, D), lambda b, i: (b, 0, 0)),
        pl.BlockSpec((D, LANES), lambda b, i: (0, 0)),
        pl.BlockSpec((D, LANES), lambda b, i: (0, 0)),
        pl.BlockSpec((1, LANES), lambda b, i: (0, 0)),
    ]
    return pl.pallas_call(
        functools.partial(_post_kernel, n_y=n_y),
        grid=(B, S // tm),
        in_specs=in_specs,
        out_specs=[
            pl.BlockSpec((None, tm, D), lambda b, i: (b, i, 0)),
            pl.BlockSpec((None, tm * (D // 2) // LANES, LANES), lambda b, i: (b, i, 0)),
            pl.BlockSpec((None, N_EXPERTS, tm), lambda b, i: (b, 0, i)),
        ],
        out_shape=[
            jax.ShapeDtypeStruct((B, S, D), F32),
            jax.ShapeDtypeStruct((B, S * (D // 2) // LANES, LANES), jnp.uint32),
            jax.ShapeDtypeStruct((B, N_EXPERTS, S), F32),
        ],
        compiler_params=_cparams(("arbitrary", "arbitrary")),
        name="post",
    )(*ys, *ws, x, gate, g2, sc2, sh2, rwh, rwl, rb)


def _select_consts(S):
    nt = S // LANES
    E = N_EXPERTS
    li = np.arange(LANES)
    u_incl = (li[:, None] <= li[None, :]).astype(np.float32)
    u_strict = (li[:, None] < li[None, :]).astype(np.float32)
    r = np.arange(nt * E)
    tt, ee = r // E, r % E
    l_strict = ((ee[:, None] == ee[None, :]) & (tt[None, :] < tt[:, None])).astype(np.float32)
    tind = (np.arange(S)[:, None] // LANES == li[None, :]).astype(np.float32)
    f = lambda a: jnp.asarray(a, F32).astype(BF16)
    return f(u_incl), f(u_strict), f(l_strict), f(tind)


def _select_kernel(aff_ref, ui_ref, us_ref, ls_ref, tind_ref, idx_ref, gate_ref, ts_ref,
                   affx_ref, lcs_ref, cex_ref, cin_ref, *, S, cap):
    E = N_EXPERTS
    nt = S // LANES
    aff = aff_ref[...]
    v = pltpu.bitcast(aff, I32)
    capf = jnp.float32(cap)

    def count(mask):
        return jnp.sum(jnp.where(mask, 1.0, 0.0), axis=1, keepdims=True)

    def search(i, t):
        cand = t | jnp.left_shift(jnp.int32(1), 30 - i)
        return jnp.where(count(v >= cand) >= capf, cand, t)

    thr = lax.fori_loop(0, 31, search, jnp.zeros((E, 1), I32))
    gt = v > thr
    eq = v == thr
    need = capf - count(gt)

    def to_tiles(m):
        return jnp.concatenate([m[:, LANES * t:LANES * (t + 1)] for t in range(nt)], axis=0)

    def prefix(mt):
        lcs = jnp.dot(mt.astype(BF16), ui_ref[...], preferred_element_type=F32)
        off = jnp.dot(ls_ref[...], lcs.astype(BF16), preferred_element_type=F32)[:, LANES - 1:LANES]
        return lcs, off

    eqt = to_tiles(jnp.where(eq, 1.0, 0.0))
    lcs_e, off_e = prefix(eqt)
    rank_t = lcs_e + off_e - eqt
    rank = jnp.concatenate([rank_t[E * t:E * (t + 1), :] for t in range(nt)], axis=1)
    sel = gt | (eq & (rank < need))
    self_ = jnp.where(sel, 1.0, 0.0)

    tot_et = jnp.dot(self_.astype(BF16), tind_ref[...], preferred_element_type=F32)
    ts_ref[...] = jnp.dot(tot_et.astype(BF16), us_ref[...], preferred_element_type=F32).astype(I32)

    selt = to_tiles(self_)
    lcs, off = prefix(selt)
    nrow = nt * E
    zeros = jnp.zeros((nrow, LANES), F32)
    pad_rows = LANES * E - nrow
    lcs_ref[0:nrow, :] = lcs
    lcs_ref[nrow:LANES * E, :] = jnp.zeros((pad_rows, LANES), F32)
    affx_ref[0:nrow, :] = to_tiles(aff)
    affx_ref[nrow:LANES * E, :] = jnp.zeros((pad_rows, LANES), F32)
    cex_ref[...] = off + zeros
    cin_ref[...] = off + lcs[:, LANES - 1:LANES] + zeros

    jrow = lax.broadcasted_iota(I32, (1, cap), 1).astype(F32)
    trow = lax.broadcasted_iota(I32, (LANES, 1), 0).astype(F32)

    def per_expert(e, carry):
        lc = lcs_ref[pl.ds(e, LANES, stride=E), :]
        af = affx_ref[pl.ds(e, LANES, stride=E), :]
        cx = cex_ref[pl.ds(e, nt, stride=E), :][:, 0:1]
        ci = cin_ref[pl.ds(e, nt, stride=E), :][:, 0:1]
        le = jnp.where(ci <= jrow, 1.0, 0.0)
        tj = jnp.sum(le, axis=0, keepdims=True)
        r = jrow - jnp.sum(le * (ci - cx), axis=0, keepdims=True)
        onehot = jnp.where(trow == tj, 1.0, 0.0).astype(BF16)
        g = jnp.dot(lc.T.astype(BF16), onehot, preferred_element_type=F32)
        lo = jnp.sum(jnp.where(g <= r, 1.0, 0.0), axis=0, keepdims=True)
        idx_ref[pl.ds(e, 1), :] = (tj * LANES + lo).astype(I32)
        aft = af.T
        a1 = aft.astype(BF16)
        r1 = aft - a1.astype(F32)
        a2 = r1.astype(BF16)
        a3 = (r1 - a2.astype(F32)).astype(BF16)
        ag = (jnp.dot(a1, onehot, preferred_element_type=F32) + jnp.dot(a2, onehot, preferred_element_type=F32)
              + jnp.dot(a3, onehot, preferred_element_type=F32))
        gate_ref[pl.ds(e, 1), :] = jnp.sum(jnp.where(trow == lo, ag, 0.0), axis=0, keepdims=True)
        return carry

    lax.fori_loop(0, E, per_expert, 0)


def select_tokens(aff_t):
    B, E, S = aff_t.shape
    cap = CAPACITY_FACTOR * S // E
    nt = S // LANES
    consts = _select_consts(S)
    const_specs = [pl.BlockSpec(c.shape, lambda b: (0, 0)) for c in consts]
    return pl.pallas_call(
        functools.partial(_select_kernel, S=S, cap=cap),
        grid=(B,),
        in_specs=[pl.BlockSpec((None, E, S), lambda b: (b, 0, 0))] + const_specs,
        out_specs=[
            pl.BlockSpec((None, E, cap), lambda b: (b, 0, 0)),
            pl.BlockSpec((None, E, cap), lambda b: (b, 0, 0)),
            pl.BlockSpec((None, E, LANES), lambda b: (b, 0, 0)),
        ],
        out_shape=[
            jax.ShapeDtypeStruct((B, E, cap), I32),
            jax.ShapeDtypeStruct((B, E, cap), F32),
            jax.ShapeDtypeStruct((B, E, LANES), I32),
        ],
        scratch_shapes=[
            pltpu.VMEM((LANES * E, LANES), F32),
            pltpu.VMEM((LANES * E, LANES), F32),
            pltpu.VMEM((nt * E, LANES), F32),
            pltpu.VMEM((nt * E, LANES), F32),
        ],
        compiler_params=_cparams(("arbitrary",)),
        name="select",
    )(aff_t, *consts)


GATHER_UNROLL = 8
FFN_ROWS = 256


def _gather_kernel(idx_ref, hp_ref, xg_ref, *, cap):
    nl = xg_ref.shape[0] // cap

    def gather(t, c):
        j0 = t * GATHER_UNROLL
        rows = [hp_ref[pl.ds(pl.multiple_of(idx_ref[0, j0 + r] * nl, nl), nl), :] for r in range(GATHER_UNROLL)]
        for r in range(GATHER_UNROLL):
            xg_ref[pl.ds(pl.multiple_of((j0 + r) * nl, nl), nl), :] = rows[r]
        return c

    lax.fori_loop(0, cap // GATHER_UNROLL, gather, 0)


def gather_rows(idx, hp, nl):
    B, E, cap = idx.shape
    hrows = hp.shape[1]
    return pl.pallas_call(
        functools.partial(_gather_kernel, cap=cap),
        grid=(B, E),
        in_specs=[
            pl.BlockSpec((None, 1, cap), lambda b, e: (b * E + e, 0, 0), memory_space=pltpu.SMEM),
            pl.BlockSpec((None, hrows, LANES), lambda b, e: (b, 0, 0), pipeline_mode=pl.Buffered(1)),
        ],
        out_specs=pl.BlockSpec((None, cap * nl, LANES), lambda b, e: (b * E + e, 0, 0)),
        out_shape=jax.ShapeDtypeStruct((B * E, cap * nl, LANES), jnp.uint32),
        compiler_params=_cparams(("arbitrary", "arbitrary")),
        name="gather",
    )(idx.reshape(B * E, 1, cap), hp)


def _ffn_kernel(gate_ref, xg_ref, wg_ref, wu_ref, wd_ref, y_ref, wgb_ref, wub_ref, wdb_ref, *, cap):
    @pl.when(pl.program_id(1) == 0)
    def _():
        wgb_ref[...] = wg_ref[...].astype(BF16)
        wub_ref[...] = wu_ref[...].astype(BF16)
        wdb_ref[...] = wd_ref[...].astype(BF16)

    half = wg_ref.shape[0] // 2
    nl = half // LANES
    ntile = nl
    gcol = jnp.broadcast_to(gate_ref[...], (LANES, cap)).T[:, 0:1]
    rows = min(FFN_ROWS, cap)
    for r in range(cap // rows):
        sl = slice(r * rows, (r + 1) * rows)
        xp = jnp.concatenate([xg_ref[pl.ds(r * rows * nl + t, rows, stride=nl), :] for t in range(nl)],
                             axis=1)
        lo = pltpu.bitcast(xp << 16, F32).astype(BF16)
        hi = pltpu.bitcast(xp & jnp.uint32(0xFFFF0000), F32).astype(BF16)
        a = (jnp.dot(lo, wgb_ref[:half, :], preferred_element_type=F32)
             + jnp.dot(hi, wgb_ref[half:, :], preferred_element_type=F32))
        u = (jnp.dot(lo, wub_ref[:half, :], preferred_element_type=F32)
             + jnp.dot(hi, wub_ref[half:, :], preferred_element_type=F32))
        hid = ((a * jax.nn.sigmoid(a)) * u).astype(BF16)
        y = (jnp.dot(hid, wdb_ref[...], preferred_element_type=F32) * gcol[sl, :]).astype(BF16)
        yp = ((pltpu.bitcast(y[:, :half].astype(F32), jnp.uint32) >> 16)
              | (pltpu.bitcast(y[:, half:].astype(F32), jnp.uint32) & jnp.uint32(0xFFFF0000)))
        for t in range(ntile):
            y_ref[pl.ds(r * rows * ntile + t, rows, stride=ntile), :] = yp[:, t * LANES:(t + 1) * LANES]


def expert_ffn(gate, xg, wg, wu, wd, layer):
    B, E, cap = gate.shape
    D, Fdim = wg.shape[-2:]
    xrows = xg.shape[1]
    return pl.pallas_call(
        functools.partial(_ffn_kernel, cap=cap),
        grid=(E, B),
        in_specs=[
            pl.BlockSpec((None, 1, cap), lambda e, b: (b * E + e, 0, 0)),
            pl.BlockSpec((None, xrows, LANES), lambda e, b: (b * E + e, 0, 0)),
            pl.BlockSpec((None, None, D, Fdim), lambda e, b: (layer, e, 0, 0)),
            pl.BlockSpec((None, None, D, Fdim), lambda e, b: (layer, e, 0, 0)),
            pl.BlockSpec((None, None, Fdim, D), lambda e, b: (layer, e, 0, 0)),
        ],
        out_specs=pl.BlockSpec((None, xrows, LANES), lambda e, b: (b * E + e, 0, 0)),
        out_shape=jax.ShapeDtypeStruct((B * E, xrows, LANES), jnp.uint32),
        scratch_shapes=[pltpu.VMEM((D, Fdim), BF16), pltpu.VMEM((D, Fdim), BF16), pltpu.VMEM((Fdim, D), BF16)],
        compiler_params=_cparams(("arbitrary", "arbitrary")),
        name="ffn",
    )(gate.reshape(B * E, 1, cap), xg, wg, wu, wd)


COMBINE_EG = 16
COMBINE_BR = 128
COMBINE_NBUF = 6
COMBINE_UNROLL = 4


def _combine_kernel(idx_ref, ts_ref, x_ref, gate_ref, fg_ref, y_hbm, o_ref, ybuf_ref, sem, be_ref, bk_ref, acc_ref,
                    *, tc, cap, n_e, final_norm_out):
    b = pl.program_id(0)
    c = pl.program_id(1)
    g = pl.program_id(2)
    eg, br = COMBINE_EG, COMBINE_BR
    tiles = tc // LANES
    base = c * tc
    nl = x_ref.shape[1] // LANES
    ny = nl // 2

    @pl.when(g == 0)
    def _():
        acc_ref[...] = jnp.zeros_like(acc_ref)

    def slot_range(el):
        return ts_ref[el * LANES + c * tiles], ts_ref[el * LANES + (c + 1) * tiles]

    def build(el, n):
        lo, hi = slot_range(el)
        k0 = lo // br
        k1 = jnp.where(hi > lo, (hi + br - 1) // br, k0)

        def push(k, m):
            be_ref[m] = el
            bk_ref[m] = k
            return m + 1

        return lax.fori_loop(k0, k1, push, n)

    nblk = lax.fori_loop(0, eg, build, 0)

    def blk_copy(i, slot):
        row = b * n_e + g * eg + be_ref[i]
        return pltpu.make_async_copy(y_hbm.at[row, pl.ds(bk_ref[i] * (br * ny), br * ny), :], ybuf_ref.at[slot],
                                     sem.at[slot])

    ahead = COMBINE_NBUF - 1
    for i0 in range(ahead):
        @pl.when(i0 < nblk)
        def _():
            blk_copy(i0, i0).start()

    def per_block(i, carry):
        slot = i % COMBINE_NBUF

        @pl.when(i + ahead < nblk)
        def _():
            blk_copy(i + ahead, (i + ahead) % COMBINE_NBUF).start()

        blk_copy(i, slot).wait()
        el = be_ref[i]
        r0 = bk_ref[i] * br
        lo, hi = slot_range(el)
        j0 = jnp.maximum(lo, r0)
        j1 = jnp.minimum(hi, r0 + br)

        def unpack(w):
            return jnp.concatenate([pltpu.bitcast(w << 16, F32),
                                    pltpu.bitcast(w & jnp.uint32(0xFFFF0000), F32)], axis=0)

        def add_rows(j, n):
            offs = [pl.multiple_of((idx_ref[el * cap + j + r] - base) * nl, nl) for r in range(n)]
            vals = [acc_ref[pl.ds(offs[r], nl), :]
                    + unpack(ybuf_ref[slot, pl.ds(pl.multiple_of((j + r - r0) * ny, ny), ny), :]) for r in range(n)]
            for r in range(n):
                acc_ref[pl.ds(offs[r], nl), :] = vals[r]

        ngrp = (j1 - j0) // COMBINE_UNROLL

        def grp(t, cc):
            add_rows(j0 + t * COMBINE_UNROLL, COMBINE_UNROLL)
            return cc

        lax.fori_loop(0, ngrp, grp, 0)

        def one(j, cc):
            add_rows(j, 1)
            return cc

        lax.fori_loop(j0 + ngrp * COMBINE_UNROLL, j1, one, 0)
        return carry

    lax.fori_loop(0, nblk, per_block, 0)

    @pl.when(g == n_e // eg - 1)
    def _():
        ss = jnp.zeros((tc, 1), F32)
        for t in range(nl):
            ls = slice(t * LANES, (t + 1) * LANES)
            xn = x_ref[:, ls] + gate_ref[:, ls] * acc_ref[pl.ds(t, tc, stride=nl), :]
            o_ref[:, ls] = xn
            if final_norm_out:
                ss = ss + jnp.sum(xn * xn, axis=-1, keepdims=True)
        if final_norm_out:
            o_ref[...] = o_ref[...] * lax.rsqrt(ss / (nl * LANES) + EPS) * fg_ref[...]


def combine(idx, tstart, x, gate, y, final_g=None, tc=2048):
    B, S, D = x.shape
    _, E, cap = idx.shape
    tc = min(tc, S)
    eg = COMBINE_EG
    ng = E // eg
    max_blocks = eg * (cap // COMBINE_BR + 1)
    fg = jnp.ones((1, D), F32) if final_g is None else final_g
    return pl.pallas_call(
        functools.partial(_combine_kernel, tc=tc, cap=cap, n_e=E, final_norm_out=final_g is not None),
        grid=(B, S // tc, ng),
        in_specs=[
            pl.BlockSpec((eg * cap,), lambda b, c, g: (b * ng + g,), memory_space=pltpu.SMEM),
            pl.BlockSpec((eg * LANES,), lambda b, c, g: (b * ng + g,), memory_space=pltpu.SMEM),
            pl.BlockSpec((None, tc, D), lambda b, c, g: (b, c, 0)),
            pl.BlockSpec((None, 1, D), lambda b, c, g: (b, 0, 0)),
            pl.BlockSpec((1, D), lambda b, c, g: (0, 0)),
            pl.BlockSpec(memory_space=pl.ANY),
        ],
        out_specs=pl.BlockSpec((None, tc, D), lambda b, c, g: (b, c, 0)),
        out_shape=jax.ShapeDtypeStruct((B, S, D), F32),
        scratch_shapes=[
            pltpu.VMEM((COMBINE_NBUF, COMBINE_BR * D // (2 * LANES), LANES), jnp.uint32),
            pltpu.SemaphoreType.DMA((COMBINE_NBUF,)),
            pltpu.SMEM((max_blocks,), I32),
            pltpu.SMEM((max_blocks,), I32),
            pltpu.VMEM((tc * D // LANES, LANES), F32),
        ],
        compiler_params=_cparams(("arbitrary", "arbitrary", "arbitrary")),
        name="combine",
    )(idx.reshape(-1), tstart.reshape(-1), x, gate, fg, y)


def _pad_lanes(a):
    return jnp.pad(a, ((0, 0), (0, LANES - a.shape[-1])))


def moe_block(x, h, aff_t, gate, wg, wu, wd, layer, final_g=None):
    idx, g, tstart = select_tokens(aff_t)
    xg = gather_rows(idx, h, wg.shape[-2] // 2 // LANES)
    y = expert_ffn(g, xg, wg, wu, wd, layer)
    return combine(idx, tstart, x, gate, y, final_g)


def kernel(x, c, positions, ada_w, ada_b, mix_norm_g, ffn_norm_g, fc_w_in, conv_w, conv_b, conv_ln_g, conv_ln_b, fc_w_out, attn_w_qkv, attn_sink, attn_w_out, router_w, router_b, moe_w_gate, moe_w_up, moe_w_down, final_norm_g):
    B, S, D = x.shape
    depth = ada_w.shape[0]
    mod = adaln_all(c, ada_w.reshape(depth * 2, D, 3 * D), ada_b.reshape(depth * 2, 1, 3 * D))
    mod = mod.reshape(depth, 2, B, 1, 3 * D)
    cos, sin = rope_tables(positions)
    for l in range(depth):
        i = l // 2
        shift, scale, gate = (mod[l, 0, :, :, k * D:(k + 1) * D] for k in range(3))
        shift2, scale2, gate2 = (mod[l, 1, :, :, k * D:(k + 1) * D] for k in range(3))
        g1 = mix_norm_g[l][None, :]
        if l % 2 == 0:
            uf, glu = even_in(x, g1, scale, shift, fc_w_in[i].astype(BF16))
            yf = fourier_mix(uf)
            yc = conv_ln_swish(glu, conv_w[i], conv_b[i][None, :], conv_ln_g[i][None, :], conv_ln_b[i][None, :])
            wo = fc_w_out[i].astype(BF16)
            ys, ws = [yf, yc], [wo[:FOURIER_WIDTH], wo[FOURIER_WIDTH:]]
        else:
            q, k, v = odd_in(x, g1, scale, shift, attn_w_qkv[i].astype(BF16), cos, sin)
            ys, ws = [attention(q, k, v, attn_sink[i])], [attn_w_out[i].astype(BF16)]
        rw = _pad_lanes(router_w[l])
        rwh = rw.astype(BF16)
        rwl = (rw - rwh.astype(F32)).astype(BF16)
        x, h, aff_t = post_mixer(ys, ws, x, gate, ffn_norm_g[l][None, :], scale2, shift2,
                                 rwh, rwl, _pad_lanes(router_b[l][None, :]))
        x = moe_block(x, h, aff_t, gate2, moe_w_gate, moe_w_up, moe_w_down, l,
                      final_norm_g[None, :] if l == depth - 1 else None)
    return x
```
